```python
import math
import jax, jax.numpy as jnp
from jax import lax
import numpy as np

D_MODEL = 4096
BATCH = 4
SEQ = 2048
DEPTH = 4
DEC_BATCH = 8
DEC_SEQ = 4
PAST_LEN = 8192
PAGE_SIZE = 128

N_MIXERS = 2
N_HGRN_LAYERS = (DEPTH + N_MIXERS - 1) // N_MIXERS
N_MOBA_LAYERS = DEPTH // N_MIXERS
HEAD_DIM = 128
D_MAIN = 3 * D_MODEL // 4
N_HEADS = D_MAIN // HEAD_DIM
D_CROSS = D_MODEL - D_MAIN
N_MEM_HEADS = 4
MEM_HEAD_DIM = D_CROSS // N_MEM_HEADS
MEM_LEN = 256
D_MIX = D_MAIN + D_CROSS
IN_WIDTH = 3 * D_MAIN + D_CROSS + D_MIX
HGRN_CHUNK = 32
MOBA_BLOCK = 256
MOBA_TOPK = 3
MOBA_Q_CHUNK = 8
ROPE_THETA = 10000.0
RMS_EPS = 1e-6
MIN_FORGET = 1e-30

kernel_name = "hgrn2_moba_memxattn_decoder_step"


def rmsnorm(x, g):
    xf = x.astype(jnp.float32)
    y = xf * lax.rsqrt(jnp.mean(xf * xf, axis=-1, keepdims=True) + RMS_EPS)
    return (y * g.astype(jnp.float32)).astype(x.dtype)


def rope(x, pos):
    half = HEAD_DIM // 2
    inv = ROPE_THETA ** (-jnp.arange(half, dtype=jnp.float32) / half)
    ang = pos.astype(jnp.float32)[:, None] * inv[None, :]
    cos = jnp.cos(ang)[None, :, None, :]
    sin = jnp.sin(ang)[None, :, None, :]
    xf = x.astype(jnp.float32)
    x1, x2 = xf[..., :half], xf[..., half:]
    return jnp.concatenate([x1 * cos - x2 * sin, x2 * cos + x1 * sin], axis=-1).astype(x.dtype)


def gla_chunkwise(q, k, v, log_f, s0):
    b, t, h, dk = q.shape
    dv = v.shape[-1]
    c = math.gcd(t, HGRN_CHUNK)
    n = t // c

    def chunks(a):
        return a.reshape(b, n, c, h, a.shape[-1]).transpose(1, 0, 3, 2, 4)

    cum = jnp.cumsum(chunks(log_f), axis=3)
    tri = jnp.tril(jnp.ones((c, c), dtype=bool))[:, :, None]

    def step(s, xs):
        qc, kc, vc, bc = xs
        inter = jnp.einsum('bhtd,bhde->bhte', qc * jnp.exp(bc), s)
        diff = bc[:, :, :, None, :] - bc[:, :, None, :, :]
        decay = jnp.where(tri, jnp.exp(jnp.where(tri, diff, 0.0)), 0.0)
        att = jnp.einsum('bhtd,bhsd,bhtsd->bhts', qc, kc, decay)
        intra = jnp.einsum('bhts,bhse->bhte', att, vc)
        last = bc[:, :, -1]
        s_new = jnp.exp(last)[..., None] * s + jnp.einsum(
            'bhsd,bhse->bhde', kc * jnp.exp(last[:, :, None] - bc), vc)
        return s_new, inter + intra

    s_fin, o = lax.scan(step, s0, (chunks(q), chunks(k), chunks(v), cum))
    o = o.transpose(1, 0, 3, 2, 4).reshape(b, t, h, dv)
    return o, s_fin


def hgrn2_mixer(main, s0, lb, out_g):
    b, t, _ = main.shape
    mf = main.astype(jnp.float32)
    q, fx, inp = jnp.split(mf, 3, axis=-1)
    lb = lb.astype(jnp.float32)
    f = lb + (1.0 - lb) * jax.nn.sigmoid(fx)
    log_f = jnp.log(jnp.maximum(f, MIN_FORGET))
    k = (1.0 - lb) * jax.nn.sigmoid(-fx)

    def heads(a):
        return a.reshape(b, t, N_HEADS, HEAD_DIM)

    o, s_fin = gla_chunkwise(heads(q), heads(k), heads(inp), heads(log_f), s0.astype(jnp.float32))
    o = rmsnorm(o, out_g).reshape(b, t, D_MAIN).astype(main.dtype)
    return o, s_fin.astype(s0.dtype)


def moba_attention(q, k, v, q_pos):
    b, tq, h, dh = q.shape
    L = k.shape[1]
    nblk = -(-L // MOBA_BLOCK)
    pad = nblk * MOBA_BLOCK - L
    k_p = jnp.pad(k, ((0, 0), (0, pad), (0, 0), (0, 0)))
    v_p = jnp.pad(v, ((0, 0), (0, pad), (0, 0), (0, 0)))
    k_blk = k_p.reshape(b, nblk, MOBA_BLOCK, h, dh)
    v_blk = v_p.reshape(b, nblk, MOBA_BLOCK, h, dh)
    k_mean = jnp.mean(k_blk.astype(jnp.float32), axis=2)
    gate = jnp.einsum('bthd,bnhd->bhtn', q.astype(jnp.float32), k_mean)
    own = q_pos // MOBA_BLOCK
    cand = jnp.arange(nblk)[None, :] < own[:, None]
    gate = jnp.where(cand[None, None], gate, -jnp.inf)
    if nblk < MOBA_TOPK:
        gate = jnp.pad(gate, ((0, 0), (0, 0), (0, 0), (0, MOBA_TOPK - nblk)), constant_values=-jnp.inf)
    vals, idx = lax.top_k(gate, MOBA_TOPK)
    valid = jnp.isfinite(vals)
    idx = jnp.minimum(idx, nblk - 1).astype(jnp.int32)
    own_b = jnp.broadcast_to(own.astype(jnp.int32)[None, None, :, None], (b, h, tq, 1))
    sel = jnp.concatenate([idx, own_b], axis=-1)
    sel_valid = jnp.concatenate([valid, jnp.ones((b, h, tq, 1), dtype=bool)], axis=-1)

    k_bh = k_blk.transpose(0, 3, 1, 2, 4)
    v_bh = v_blk.transpose(0, 3, 1, 2, 4)
    qh = q.transpose(0, 2, 1, 3)
    qc = math.gcd(tq, MOBA_Q_CHUNK)
    n = tq // qc

    def to_chunks(a):
        return jnp.moveaxis(a.reshape(a.shape[0], a.shape[1], n, qc, *a.shape[3:]), 2, 0)

    bi = jnp.arange(b)[:, None, None, None]
    hi = jnp.arange(h)[None, :, None, None]
    offs = jnp.arange(MOBA_BLOCK, dtype=jnp.int32)
    scale = dh ** -0.5

    def attend(args):
        qb, selb, validb, posb = args
        k_sel = k_bh[bi, hi, selb]
        v_sel = v_bh[bi, hi, selb]
        kpos = selb[..., None] * MOBA_BLOCK + offs
        mask = validb[..., None] & (kpos <= posb[None, None, :, None, None])
        logits = jnp.einsum('bhtd,bhtrkd->bhtrk', qb, k_sel).astype(jnp.float32) * scale
        logits = jnp.where(mask, logits, -jnp.inf)
        p = jax.nn.softmax(logits.reshape(b, h, qc, -1), axis=-1).reshape(logits.shape)
        return jnp.einsum('bhtrk,bhtrkd->bhtd', p.astype(v_sel.dtype), v_sel)

    out = lax.map(attend, (to_chunks(qh), to_chunks(sel), to_chunks(sel_valid), q_pos.reshape(n, qc)))
    out = jnp.moveaxis(out, 0, 2).reshape(b, h, tq, dh)
    return out.transpose(0, 2, 1, 3)


def mem_kv(mem, g, w):
    b, m, _ = mem.shape
    kv = rmsnorm(mem, g) @ w
    mk, mv = jnp.split(kv, 2, axis=-1)
    return (mk.reshape(b, m, N_MEM_HEADS, MEM_HEAD_DIM), mv.reshape(b, m, N_MEM_HEADS, MEM_HEAD_DIM))


def cross_attention(cq, mk, mv):
    b, t, _ = cq.shape
    qh = cq.reshape(b, t, N_MEM_HEADS, MEM_HEAD_DIM)
    logits = jnp.einsum('bthd,bmhd->bhtm', qh, mk).astype(jnp.float32) * (MEM_HEAD_DIM ** -0.5)
    p = jax.nn.softmax(logits, axis=-1).astype(mv.dtype)
    return jnp.einsum('bhtm,bmhd->bthd', p, mv).reshape(b, t, D_CROSS)


def setup_inputs(seed: int = 0) -> dict:
    key = jax.random.key(seed)
    ks = jax.random.split(key, 20)
    n_pages = PAST_LEN // PAGE_SIZE
    n_used = DEC_BATCH * n_pages
    n_phys = n_used + (n_used + 3) // 4
    page_table = jax.random.permutation(ks[0], n_phys)[:n_used].reshape(DEC_BATCH, n_pages).astype(jnp.int32)
    f32 = jnp.float32
    nrm = jax.random.normal
    return {
        "x_prompt": nrm(ks[1], (BATCH, SEQ, D_MODEL), f32),
        "x_sample": nrm(ks[2], (DEC_BATCH, DEC_SEQ, D_MODEL), f32),
        "cache_moba_k": nrm(ks[3], (N_MOBA_LAYERS, n_phys, PAGE_SIZE, N_HEADS, HEAD_DIM), f32),
        "cache_moba_v": nrm(ks[4], (N_MOBA_LAYERS, n_phys, PAGE_SIZE, N_HEADS, HEAD_DIM), f32),
        "state_hgrn": 0.5 * nrm(ks[5], (N_HGRN_LAYERS, DEC_BATCH, N_HEADS, HEAD_DIM, HEAD_DIM), f32),
        "cache_mem_k": nrm(ks[6], (DEPTH, DEC_BATCH, MEM_LEN, N_MEM_HEADS, MEM_HEAD_DIM), f32),
        "cache_mem_v": nrm(ks[7], (DEPTH, DEC_BATCH, MEM_LEN, N_MEM_HEADS, MEM_HEAD_DIM), f32),
        "page_table": page_table,
        "mem_prompt": nrm(ks[8], (BATCH, MEM_LEN, D_MODEL), f32),
        "norm_g": 1.0 + 0.01 * nrm(ks[9], (DEPTH, D_MODEL), f32),
        "w_in": nrm(ks[10], (DEPTH, D_MODEL, IN_WIDTH), f32) * (D_MODEL ** -0.5),
        "w_out": nrm(ks[11], (DEPTH, D_MIX, D_MODEL), f32) * (D_MIX ** -0.5),
        "hgrn_lb_logits": 0.1 * nrm(ks[12], (N_HGRN_LAYERS, D_MAIN), f32),
        "hgrn_out_g": 1.0 + 0.01 * nrm(ks[13], (N_HGRN_LAYERS, HEAD_DIM), f32),
        "mem_norm_g": 1.0 + 0.01 * nrm(ks[14], (DEPTH, D_MODEL), f32),
        "w_mem_kv": nrm(ks[15], (DEPTH, D_MODEL, 2 * D_CROSS), f32) * (D_MODEL ** -0.5),
        "final_norm_g": 1.0 + 0.01 * nrm(ks[16], (D_MODEL,), f32),
    }


def reference(x_prompt, x_sample, cache_moba_k, cache_moba_v, state_hgrn, cache_mem_k, cache_mem_v,
              page_table, mem_prompt, norm_g, w_in, w_out, hgrn_lb_logits, hgrn_out_g, mem_norm_g,
              w_mem_kv, final_norm_g):
    bp, tp, _ = x_prompt.shape
    bs, ts, _ = x_sample.shape
    n_pages = page_table.shape[1]
    past_len = n_pages * PAGE_SIZE
    pos_p = jnp.arange(tp, dtype=jnp.int32)
    pos_s = past_len + jnp.arange(ts, dtype=jnp.int32)

    lb_w = jax.nn.softmax(hgrn_lb_logits.astype(jnp.float32), axis=0)
    lower_bounds = jnp.cumsum(lb_w, axis=0) - lb_w[0:1]

    xp, xs = x_prompt, x_sample
    moba_k_p, moba_v_p, moba_k_s, moba_v_s = [], [], [], []
    hgrn_p, hgrn_s, memk_p, memv_p = [], [], [], []

    def split_proj(hproj):
        return (hproj[..., :3 * D_MAIN], hproj[..., 3 * D_MAIN:3 * D_MAIN + D_CROSS],
                hproj[..., 3 * D_MAIN + D_CROSS:])

    for l in range(DEPTH):
        j = l // N_MIXERS
        mp, cqp, gp = split_proj(rmsnorm(xp, norm_g[l]) @ w_in[l])
        ms, cqs, gs = split_proj(rmsnorm(xs, norm_g[l]) @ w_in[l])
        if l % N_MIXERS == 0:
            s0_p = jnp.zeros((bp, N_HEADS, HEAD_DIM, HEAD_DIM), dtype=xp.dtype)
            op, sp = hgrn2_mixer(mp, s0_p, lower_bounds[j], hgrn_out_g[j])
            o_s, ss = hgrn2_mixer(ms, state_hgrn[j], lower_bounds[j], hgrn_out_g[j])
            hgrn_p.append(sp)
            hgrn_s.append(ss)
        else:
            qp, kp, vp = [a.reshape(bp, tp, N_HEADS, HEAD_DIM) for a in jnp.split(mp, 3, axis=-1)]
            qp, kp = rope(qp, pos_p), rope(kp, pos_p)
            op = moba_attention(qp, kp, vp, pos_p).reshape(bp, tp, D_MAIN)
            qs, ks_, vs = [a.reshape(bs, ts, N_HEADS, HEAD_DIM) for a in jnp.split(ms, 3, axis=-1)]
            qs, ks_ = rope(qs, pos_s), rope(ks_, pos_s)
            past_k = cache_moba_k[j][page_table].reshape(bs, past_len, N_HEADS, HEAD_DIM)
            past_v = cache_moba_v[j][page_table].reshape(bs, past_len, N_HEADS, HEAD_DIM)
            o_s = moba_attention(qs, jnp.concatenate([past_k, ks_], axis=1),
                                 jnp.concatenate([past_v, vs], axis=1), pos_s).reshape(bs, ts, D_MAIN)
            moba_k_p.append(kp)
            moba_v_p.append(vp)
            moba_k_s.append(ks_)
            moba_v_s.append(vs)
        mkp, mvp = mem_kv(mem_prompt, mem_norm_g[l], w_mem_kv[l])
        memk_p.append(mkp)
        memv_p.append(mvp)
        cp = cross_attention(cqp, mkp, mvp)
        cs = cross_attention(cqs, cache_mem_k[l], cache_mem_v[l])
        xp = xp + (jnp.concatenate([op, cp], axis=-1) * jax.nn.silu(gp)) @ w_out[l]
        xs = xs + (jnp.concatenate([o_s, cs], axis=-1) * jax.nn.silu(gs)) @ w_out[l]

    y_prompt = rmsnorm(xp, final_norm_g)
    y_sample = rmsnorm(xs, final_norm_g)
    new_moba_k_prompt = jnp.stack(moba_k_p)
    new_moba_v_prompt = jnp.stack(moba_v_p)
    new_hgrn_prompt = jnp.stack(hgrn_p)
    new_mem_k_prompt = jnp.stack(memk_p)
    new_mem_v_prompt = jnp.stack(memv_p)
    new_moba_k_sample = jnp.stack(moba_k_s)
    new_moba_v_sample = jnp.stack(moba_v_s)
    new_hgrn_sample = jnp.stack(hgrn_s)
    return (y_prompt, y_sample, new_moba_k_prompt, new_moba_v_prompt, new_hgrn_prompt,
            new_mem_k_prompt, new_mem_v_prompt, new_moba_k_sample, new_moba_v_sample, new_hgrn_sample)
```

```python
import functools

import jax
import jax.numpy as jnp
from jax import lax
from jax.experimental import pallas as pl
from jax.experimental.pallas import tpu as pltpu

F32 = jnp.float32
BF16 = jnp.bfloat16

HEAD_DIM = 128
N_MEM_HEADS = 4
PAGE_SIZE = 128
MOBA_BLOCK = 256
MOBA_TOPK = 3
N_MIXERS = 2
ROPE_THETA = 10000.0
RMS_EPS = 1e-6
MIN_FORGET = 1e-30

LANES = 128
SUBLANES = 8
VMEM_LIMIT_BYTES = 52 * 1024 * 1024

GLA_CHUNK = 128
SAMPLE_ROWS = SUBLANES


def _params(*semantics):
    return pltpu.CompilerParams(dimension_semantics=semantics, vmem_limit_bytes=VMEM_LIMIT_BYTES)


def _nt_dot(a, b):
    return lax.dot_general(a.astype(BF16), b.astype(BF16), (((1,), (1,)), ((), ())),
                           preferred_element_type=F32)


def _nn_dot(a, b):
    return jnp.dot(a.astype(BF16), b.astype(BF16), preferred_element_type=F32)


def _silu(x):
    return x * jax.nn.sigmoid(x)


def _rmsnorm_body(x_ref, g_ref, o_ref):
    x = x_ref[...]
    inv = lax.rsqrt(jnp.mean(x * x, axis=-1, keepdims=True) + RMS_EPS)
    o_ref[...] = (x * inv * g_ref[...]).astype(o_ref.dtype)


def _rmsnorm(x, gains, layer, out_dtype, tm):
    m, d = x.shape
    return pl.pallas_call(
        _rmsnorm_body,
        grid=(m // tm,),
        in_specs=[pl.BlockSpec((tm, d), lambda i: (i, 0)),
                  pl.BlockSpec((None, 1, d), lambda i: (layer, 0, 0))],
        out_specs=pl.BlockSpec((tm, d), lambda i: (i, 0)),
        out_shape=jax.ShapeDtypeStruct((m, d), out_dtype),
        compiler_params=_params("arbitrary"),
        name="rmsnorm",
    )(x, gains)


def _proj_body(x_ref, w_ref, o_ref, wb_ref):
    @pl.when(pl.program_id(1) == 0)
    def _():
        wb_ref[...] = w_ref[...].astype(BF16)

    o_ref[...] = jnp.dot(x_ref[...], wb_ref[...], preferred_element_type=F32)


def _project(x, w, layer, tm, tn):
    m, kdim = x.shape
    n = w.shape[-1]
    return pl.pallas_call(
        _proj_body,
        grid=(n // tn, m // tm),
        in_specs=[pl.BlockSpec((tm, kdim), lambda j, i: (i, 0)),
                  pl.BlockSpec((None, kdim, tn), lambda j, i: (layer, 0, j))],
        out_specs=pl.BlockSpec((tm, tn), lambda j, i: (i, j)),
        out_shape=jax.ShapeDtypeStruct((m, n), F32),
        scratch_shapes=[pltpu.VMEM((kdim, tn), BF16)],
        compiler_params=_params("arbitrary", "arbitrary"),
        name="in_proj",
    )(x, w)


def _outproj_body(a_ref, c_ref, w_ref, r_ref, o_ref, wb_ref, *, d_main):
    @pl.when(pl.program_id(1) == 0)
    def _():
        wb_ref[...] = w_ref[...].astype(BF16)

    acc = jnp.dot(a_ref[...], wb_ref[:d_main, :], preferred_element_type=F32)
    acc = acc + jnp.dot(c_ref[...], wb_ref[d_main:, :], preferred_element_type=F32)
    o_ref[...] = r_ref[...] + acc


def _out_project(mix_main, mix_cross, w, layer, resid, tm, tn):
    m, d_main = mix_main.shape
    d_cross = mix_cross.shape[1]
    kdim, n = w.shape[1], w.shape[2]
    return pl.pallas_call(
        functools.partial(_outproj_body, d_main=d_main),
        grid=(n // tn, m // tm),
        in_specs=[pl.BlockSpec((tm, d_main), lambda j, i: (i, 0)),
                  pl.BlockSpec((tm, d_cross), lambda j, i: (i, 0)),
                  pl.BlockSpec((None, kdim, tn), lambda j, i: (layer, 0, j)),
                  pl.BlockSpec((tm, tn), lambda j, i: (i, j))],
        out_specs=pl.BlockSpec((tm, tn), lambda j, i: (i, j)),
        out_shape=jax.ShapeDtypeStruct((m, n), F32),
        scratch_shapes=[pltpu.VMEM((kdim, tn), BF16)],
        compiler_params=_params("arbitrary", "arbitrary"),
        name="out_proj",
    )(mix_main, mix_cross, w, resid)


def _hgrn_body(q_ref, f_ref, i_ref, g_ref, lbl_ref, og_ref, *rest, layer_j, n_valid, has_s0):
    if has_s0:
        s0_ref, o_ref, sfin_ref = rest
    else:
        o_ref, sfin_ref = rest
    c = GLA_CHUNK
    n_rows = q_ref.shape[0]

    n_layers = lbl_ref.shape[0]
    logits = [lbl_ref[i:i + 1, :] for i in range(n_layers)]
    mx = functools.reduce(jnp.maximum, logits)
    es = [jnp.exp(l - mx) for l in logits]
    tot = functools.reduce(lambda a, b: a + b, es)
    ws = [e / tot for e in es]
    csum = ws[0]
    for i in range(1, layer_j + 1):
        csum = csum + ws[i]
    lb = csum - ws[0]
    og = og_ref[...]

    row = lax.broadcasted_iota(jnp.int32, (c, HEAD_DIM), 0)
    tt = lax.broadcasted_iota(jnp.int32, (c, c), 0)
    ss = lax.broadcasted_iota(jnp.int32, (c, c), 1)

    def chunk(qc, fxc, vc, st):
        f = lb + (1.0 - lb) * jax.nn.sigmoid(fxc)
        logf = jnp.log(jnp.maximum(f, MIN_FORGET))
        kk = (1.0 - lb) * jax.nn.sigmoid(-fxc)
        if n_valid < c:
            ok = row < n_valid
            logf = jnp.where(ok, logf, 0.0)
            kk = jnp.where(ok, kk, 0.0)
        cum = logf
        s = 1
        while s < c:
            cum = cum + jnp.where(row >= s, pltpu.roll(cum, s, 0), 0.0)
            s *= 2
        att = jnp.where(tt == ss, _nt_dot(qc, kk), 0.0)
        end = cum
        m, shift = 1, 1
        while m < c:
            upper = (row & m) != 0
            end_prev = pltpu.roll(end, m, 0)
            qm = jnp.where(upper, qc * jnp.exp(jnp.where(upper, cum - end_prev, 0.0)), 0.0)
            km = jnp.where(upper, 0.0, kk * jnp.exp(jnp.where(upper, 0.0, end - cum)))
            a = _nt_dot(qm, km)
            if 2 * m < c:
                a = jnp.where((tt >> shift) == (ss >> shift), a, 0.0)
            att = att + a
            end = jnp.where(upper, end, pltpu.roll(end, c - m, 0))
            m *= 2
            shift += 1
        inter = _nt_dot(qc * jnp.exp(cum), st)
        o = inter + _nn_dot(att, vc)
        last = cum[c - 1:c, :]
        kd = kk * jnp.exp(last - cum)
        st_new = st * jnp.exp(last) + _nn_dot(vc.T, kd)
        y = o * lax.rsqrt(jnp.mean(o * o, axis=-1, keepdims=True) + RMS_EPS) * og
        return y, st_new

    st0 = s0_ref[...].T if has_s0 else jnp.zeros((HEAD_DIM, HEAD_DIM), F32)
    if n_rows < c:
        pad = jnp.zeros((c - n_rows, HEAD_DIM), F32)
        grow = lambda ref: jnp.concatenate([ref[...], pad], axis=0)
        y, st = chunk(grow(q_ref), grow(f_ref), grow(i_ref), st0)
        o_ref[...] = (y[:n_rows] * _silu(g_ref[...])).astype(o_ref.dtype)
    else:
        def body(ci, st):
            r0 = pl.multiple_of(ci * c, c)
            rows = pl.ds(r0, c)
            y, st = chunk(q_ref[rows, :], f_ref[rows, :], i_ref[rows, :], st)
            o_ref[rows, :] = (y * _silu(g_ref[rows, :])).astype(o_ref.dtype)
            return st
        st = lax.fori_loop(0, n_rows // c, body, st0)
    sfin_ref[...] = st.T


def _hgrn(proj, lb_logits, out_g, layer_j, n_seq, n_heads, d_main, gate_off, s0, n_valid):
    t = proj.shape[0] // n_seq
    hb = d_main // HEAD_DIM
    col = lambda off: (lambda b, h: (b, off + h))
    in_specs = [pl.BlockSpec((t, HEAD_DIM), col(0)),
                pl.BlockSpec((t, HEAD_DIM), col(hb)),
                pl.BlockSpec((t, HEAD_DIM), col(2 * hb)),
                pl.BlockSpec((t, HEAD_DIM), col(gate_off // HEAD_DIM)),
                pl.BlockSpec((lb_logits.shape[0], HEAD_DIM), lambda b, h: (0, h)),
                pl.BlockSpec((None, 1, HEAD_DIM), lambda b, h: (layer_j, 0, 0))]
    args = [proj, proj, proj, proj, lb_logits, out_g]
    if s0 is not None:
        in_specs.append(pl.BlockSpec((None, None, None, HEAD_DIM, HEAD_DIM),
                                     lambda b, h: (layer_j, b, h, 0, 0)))
        args.append(s0)
    return pl.pallas_call(
        functools.partial(_hgrn_body, layer_j=layer_j, n_valid=n_valid, has_s0=s0 is not None),
        grid=(n_seq, n_heads),
        in_specs=in_specs,
        out_specs=[pl.BlockSpec((t, HEAD_DIM), lambda b, h: (b, h)),
                   pl.BlockSpec((None, None, HEAD_DIM, HEAD_DIM), lambda b, h: (b, h, 0, 0))],
        out_shape=[jax.ShapeDtypeStruct((n_seq * t, d_main), BF16),
                   jax.ShapeDtypeStruct((n_seq, n_heads, HEAD_DIM, HEAD_DIM), F32)],
        compiler_params=_params("arbitrary", "arbitrary"),
        name="hgrn2",
    )(*args)


def _rope_tables(pos):
    half = HEAD_DIM // 2
    inv = ROPE_THETA ** (-jnp.arange(half, dtype=F32) / half)
    ang = pos.astype(F32)[:, None] * inv[None, :]
    cos, sin = jnp.cos(ang), jnp.sin(ang)
    return jnp.concatenate([cos, cos], axis=-1), jnp.concatenate([-sin, sin], axis=-1)


def _rope(x, cos, sin):
    return x * cos + pltpu.roll(x, HEAD_DIM // 2, 1) * sin


def _moba_body(q_ref, k_ref, v_ref, g_ref, cos_ref, sin_ref, kout_ref, o_ref,
               qf_s, qb_s, kb_s, vt_s, km_s):
    blk = MOBA_BLOCK
    nblk = q_ref.shape[0] // blk
    scale = HEAD_DIM ** -0.5

    km_s[...] = jnp.zeros_like(km_s)
    for j in range(nblk):
        rows = pl.ds(j * blk, blk)
        cos, sin = cos_ref[rows, :], sin_ref[rows, :]
        q = _rope(q_ref[rows, :], cos, sin)
        k = _rope(k_ref[rows, :], cos, sin)
        kout_ref[rows, :] = k
        qf_s[j] = q
        qb_s[j] = q.astype(BF16)
        kb_s[j] = k.astype(BF16)
        vt_s[j] = v_ref[rows, :].T.astype(BF16)
        km_s[j:j + 1, :] = jnp.sum(k, axis=0, keepdims=True) * (1.0 / blk)

    npad = km_s.shape[0]
    jidx = lax.broadcasted_iota(jnp.int32, (npad, blk), 0)
    key_i = lax.broadcasted_iota(jnp.int32, (blk, blk), 0)
    qry_i = lax.broadcasted_iota(jnp.int32, (blk, blk), 1)

    def qtile(i, carry):
        qi = qb_s[i]
        gate = lax.dot_general(km_s[...], qf_s[i], (((1,), (1,)), ((), ())),
                               precision=lax.Precision.HIGHEST, preferred_element_type=F32)
        cand = jidx < i
        gm = jnp.where(cand, gate, -jnp.inf)
        rank = jnp.zeros((npad, blk), jnp.int32)
        for jp in range(nblk):
            gj = gm[jp:jp + 1, :]
            beats = (gj > gm) | ((gj == gm) & (jp < jidx))
            rank = rank + jnp.where(beats, 1, 0)
        sel = cand & (rank < MOBA_TOPK) & (jnp.abs(gate) < jnp.inf)
        self_f = jnp.where(sel, 1.0, 0.0)

        def block(j, mask, m, l, acc):
            st = _nt_dot(kb_s[j], qi) * scale
            st = jnp.where(mask, st, -jnp.inf)
            m_new = jnp.maximum(m, jnp.max(st, axis=0, keepdims=True))
            alpha = jnp.exp(m - m_new)
            p = jnp.exp(st - m_new)
            l = alpha * l + jnp.sum(p, axis=0, keepdims=True)
            acc = alpha * acc + _nn_dot(vt_s[j], p)
            return m_new, l, acc

        init = (jnp.full((1, blk), -jnp.inf, F32), jnp.zeros((1, blk), F32),
                jnp.zeros((HEAD_DIM, blk), F32))
        state = block(i, key_i <= qry_i, *init)

        def past(j, state):
            picked = jnp.sum(jnp.where(jidx == j, self_f, 0.0), axis=0, keepdims=True) > 0.0
            return block(j, jnp.broadcast_to(picked, (blk, blk)), *state)

        _, l, acc = lax.fori_loop(0, i, past, state)
        out = (acc / l).T
        rows = pl.ds(pl.multiple_of(i * blk, blk), blk)
        o_ref[rows, :] = (out * _silu(g_ref[rows, :])).astype(o_ref.dtype)
        return carry

    lax.fori_loop(0, nblk, qtile, 0)


def _moba_prompt(proj, cos, sin, n_seq, n_heads, d_main, gate_off):
    t = proj.shape[0] // n_seq
    assert t % MOBA_BLOCK == 0
    nblk = t // MOBA_BLOCK
    hb = d_main // HEAD_DIM
    col = lambda off: (lambda b, h: (b, off + h))
    tab = pl.BlockSpec((t, HEAD_DIM), lambda b, h: (0, 0))
    return pl.pallas_call(
        _moba_body,
        grid=(n_seq, n_heads),
        in_specs=[pl.BlockSpec((t, HEAD_DIM), col(0)),
                  pl.BlockSpec((t, HEAD_DIM), col(hb)),
                  pl.BlockSpec((t, HEAD_DIM), col(2 * hb)),
                  pl.BlockSpec((t, HEAD_DIM), col(gate_off // HEAD_DIM)),
                  tab, tab],
        out_specs=[pl.BlockSpec((t, HEAD_DIM), lambda b, h: (b, h)),
                   pl.BlockSpec((t, HEAD_DIM), lambda b, h: (b, h))],
        out_shape=[jax.ShapeDtypeStruct((n_seq * t, d_main), F32),
                   jax.ShapeDtypeStruct((n_seq * t, d_main), BF16)],
        scratch_shapes=[pltpu.VMEM((nblk, MOBA_BLOCK, HEAD_DIM), F32),
                        pltpu.VMEM((nblk, MOBA_BLOCK, HEAD_DIM), BF16),
                        pltpu.VMEM((nblk, MOBA_BLOCK, HEAD_DIM), BF16),
                        pltpu.VMEM((nblk, HEAD_DIM, MOBA_BLOCK), BF16),
                        pltpu.VMEM((-(-nblk // SUBLANES) * SUBLANES, HEAD_DIM), F32)],
        compiler_params=_params("arbitrary", "arbitrary"),
        name="moba_prompt",
    )(proj, proj, proj, proj, cos, sin)


def _moba_parts_body(pt_ref, k0_ref, k1_ref, v0_ref, v1_ref, q_ref, kn_ref, vn_ref, cos_ref, sin_ref,
                     po_ref, pm_ref, pl_ref, pg_ref, knew_ref, *, n_heads, n_new, n_past_blocks):
    del pt_ref
    j = pl.program_id(1)
    scale = HEAD_DIM ** -0.5
    cos, sin = cos_ref[...], sin_ref[...]
    rows = q_ref.shape[0]

    def put(ref, h, val):
        ref[:, h * HEAD_DIM:(h + 1) * HEAD_DIM] = jnp.broadcast_to(val, (rows, HEAD_DIM))

    @pl.when(j < n_past_blocks)
    def _():
        for h in range(n_heads):
            hs = slice(h * HEAD_DIM, (h + 1) * HEAD_DIM)
            head_rows = pl.ds(h, PAGE_SIZE, stride=n_heads)
            kh = jnp.concatenate([k0_ref[head_rows, :], k1_ref[head_rows, :]], axis=0)
            vh = jnp.concatenate([v0_ref[head_rows, :], v1_ref[head_rows, :]], axis=0)
            qh = _rope(q_ref[:, hs], cos, sin)
            s = _nt_dot(qh, kh) * scale
            m = jnp.max(s, axis=1, keepdims=True)
            p = jnp.exp(s - m)
            kmean = jnp.sum(kh, axis=0, keepdims=True) * (1.0 / MOBA_BLOCK)
            po_ref[:, hs] = _nn_dot(p, vh)
            put(pm_ref, h, m)
            put(pl_ref, h, jnp.sum(p, axis=1, keepdims=True))
            put(pg_ref, h, jnp.sum(qh * kmean, axis=1, keepdims=True))

    @pl.when(j == n_past_blocks)
    def _():
        t_i = lax.broadcasted_iota(jnp.int32, (rows, 1), 0)
        for h in range(n_heads):
            hs = slice(h * HEAD_DIM, (h + 1) * HEAD_DIM)
            qh = _rope(q_ref[:, hs], cos, sin)
            kh = _rope(kn_ref[:, hs], cos, sin)
            vh = vn_ref[:, hs]
            knew_ref[:, hs] = kh
            logit = [jnp.where(t_i >= s, jnp.sum(qh * kh[s:s + 1, :], axis=1, keepdims=True) * scale, -jnp.inf)
                     for s in range(n_new)]
            m = functools.reduce(jnp.maximum, logit)
            ps = [jnp.exp(lg - m) for lg in logit]
            po_ref[:, hs] = functools.reduce(lambda a, b: a + b, [p * vh[s:s + 1, :] for s, p in enumerate(ps)])
            put(pm_ref, h, m)
            put(pl_ref, h, functools.reduce(lambda a, b: a + b, ps))
            put(pg_ref, h, jnp.zeros((rows, 1), F32))


def _moba_combine_body(po_ref, pm_ref, pl_ref, pg_ref, g_ref, o_ref, cur_s, sel_s, *, n_past_blocks):
    nb = n_past_blocks
    shape = cur_s.shape
    jidx = lax.broadcasted_iota(jnp.int32, shape, 0)
    cur_s[...] = pg_ref[:nb]
    sel_s[...] = jnp.zeros(shape, F32)
    for _ in range(MOBA_TOPK):
        cur = cur_s[...]
        best = jnp.max(cur, axis=0, keepdims=True)
        first = jnp.min(jnp.where(cur == best, jidx, nb), axis=0, keepdims=True)
        pick = jidx == first
        sel_s[...] = jnp.where(pick & (jnp.abs(best) < jnp.inf), 1.0, sel_s[...])
        cur_s[...] = jnp.where(pick, -jnp.inf, cur)
    sel = sel_s[...] > 0.0
    m_own = pm_ref[nb]
    m_all = jnp.maximum(m_own, jnp.max(jnp.where(sel, pm_ref[:nb], -jnp.inf), axis=0))
    w = jnp.where(sel, jnp.exp(pm_ref[:nb] - m_all[None]), 0.0)
    w_own = jnp.exp(m_own - m_all)
    den = jnp.sum(w * pl_ref[:nb], axis=0) + w_own * pl_ref[nb]
    num = jnp.sum(w * po_ref[:nb], axis=0) + w_own * po_ref[nb]
    o_ref[...] = ((num / den) * _silu(g_ref[...])).astype(o_ref.dtype)


def _moba_decode(proj, k_cache, v_cache, page_table, layer_j, cos, sin, n_seq, n_heads, d_main, gate_off, n_new):
    rows = SAMPLE_ROWS
    n_pages = page_table.shape[1]
    pages_per_block = MOBA_BLOCK // PAGE_SIZE
    assert pages_per_block == 2 and n_pages % pages_per_block == 0
    nb = n_pages // pages_per_block
    page_rows = PAGE_SIZE * n_heads

    def page(p):
        def index(b, j, pt):
            return (layer_j, pt[b, jnp.minimum(pages_per_block * j + p, n_pages - 1)], 0, 0)
        return pl.BlockSpec((None, None, page_rows, HEAD_DIM), index)

    tok = lambda off: pl.BlockSpec((rows, d_main), lambda b, j, pt: (b, off))
    tab = pl.BlockSpec((rows, HEAD_DIM), lambda b, j, pt: (0, 0))
    part = pl.BlockSpec((None, None, rows, d_main), lambda b, j, pt: (b, j, 0, 0))
    part_shape = jax.ShapeDtypeStruct((n_seq, nb + 1, rows, d_main), F32)
    po, pm, pls, pg, k_new = pl.pallas_call(
        functools.partial(_moba_parts_body, n_heads=n_heads, n_new=n_new, n_past_blocks=nb),
        grid_spec=pltpu.PrefetchScalarGridSpec(
            num_scalar_prefetch=1,
            grid=(n_seq, nb + 1),
            in_specs=[page(0), page(1), page(0), page(1), tok(0), tok(1), tok(2), tab, tab],
            out_specs=[part, part, part, part,
                       pl.BlockSpec((rows, d_main), lambda b, j, pt: (b, 0))]),
        out_shape=[part_shape, part_shape, part_shape, part_shape,
                   jax.ShapeDtypeStruct((n_seq * rows, d_main), F32)],
        compiler_params=_params("arbitrary", "arbitrary"),
        name="moba_decode_parts",
    )(page_table, k_cache, k_cache, v_cache, v_cache, proj, proj, proj, cos, sin)

    whole = pl.BlockSpec((None, nb + 1, rows, d_main), lambda b: (b, 0, 0, 0))
    gate_blk = 1024
    assert gate_off % gate_blk == 0 and d_main % gate_blk == 0
    n_gate = d_main // gate_blk
    gate_specs = [pl.BlockSpec((rows, gate_blk), functools.partial(lambda b, i: (b, gate_off // gate_blk + i), i=i))
                  for i in range(n_gate)]

    def body(po_ref, pm_ref, pl_ref, pg_ref, *rest):
        g_refs, (o_ref, cur_s, sel_s, g_s) = rest[:n_gate], rest[n_gate:]
        for i, g_ref in enumerate(g_refs):
            g_s[:, i * gate_blk:(i + 1) * gate_blk] = g_ref[...]
        _moba_combine_body(po_ref, pm_ref, pl_ref, pg_ref, g_s, o_ref, cur_s, sel_s, n_past_blocks=nb)

    mix = pl.pallas_call(
        body,
        grid=(n_seq,),
        in_specs=[whole, whole, whole, whole] + gate_specs,
        out_specs=pl.BlockSpec((rows, d_main), lambda b: (b, 0)),
        out_shape=jax.ShapeDtypeStruct((n_seq * rows, d_main), BF16),
        scratch_shapes=[pltpu.VMEM((nb, rows, d_main), F32), pltpu.VMEM((nb, rows, d_main), F32),
                        pltpu.VMEM((rows, d_main), F32)],
        compiler_params=_params("arbitrary"),
        name="moba_decode_combine",
    )(po, pm, pls, pg, *([proj] * n_gate))
    return mix, k_new


def _xattn_body(q_ref, k_ref, v_ref, g_ref, o_ref):
    scale = q_ref.shape[-1] ** -0.5
    s = _nt_dot(q_ref[...], k_ref[...]) * scale
    m = jnp.max(s, axis=-1, keepdims=True)
    p = jnp.exp(s - m)
    p = p / jnp.sum(p, axis=-1, keepdims=True)
    o_ref[...] = (_nn_dot(p, v_ref[...]) * _silu(g_ref[...])).astype(o_ref.dtype)


def _cross_attention(proj, mem_k, mem_v, k_spec, v_spec, n_seq, tq, q_off, gate_off, d_cross):
    t = proj.shape[0] // n_seq
    hd = d_cross // N_MEM_HEADS
    nq = t // tq
    row = lambda off: (lambda b, h, i: (b * nq + i, off + h))
    return pl.pallas_call(
        _xattn_body,
        grid=(n_seq, N_MEM_HEADS, nq),
        in_specs=[pl.BlockSpec((tq, hd), row(q_off // hd)),
                  k_spec, v_spec,
                  pl.BlockSpec((tq, hd), row(gate_off // hd))],
        out_specs=pl.BlockSpec((tq, hd), row(0)),
        out_shape=jax.ShapeDtypeStruct((n_seq * t, d_cross), BF16),
        compiler_params=_params("arbitrary", "arbitrary", "arbitrary"),
        name="mem_xattn",
    )(proj, mem_k, mem_v, proj)


def kernel(x_prompt, x_sample, cache_moba_k, cache_moba_v, state_hgrn, cache_mem_k, cache_mem_v,
           page_table, mem_prompt, norm_g, w_in, w_out, hgrn_lb_logits, hgrn_out_g, mem_norm_g,
           w_mem_kv, final_norm_g):
    bp, tp, d_model = x_prompt.shape
    bs, ts, _ = x_sample.shape
    depth = w_in.shape[0]
    d_mix = w_out.shape[1]
    d_cross = w_mem_kv.shape[2] // 2
    d_main = d_mix - d_cross
    n_heads = d_main // HEAD_DIM
    mem_len = mem_prompt.shape[1]
    mem_hd = d_cross // N_MEM_HEADS
    n_pages = page_table.shape[1]
    past_len = n_pages * PAGE_SIZE
    q_off = 3 * d_main
    gate_off = 3 * d_main + d_cross
    rows = SAMPLE_ROWS
    assert ts <= rows

    gains = norm_g.reshape(depth, 1, d_model)
    mem_gains = mem_norm_g.reshape(depth, 1, d_model)
    out_gains = hgrn_out_g.reshape(-1, 1, HEAD_DIM)

    xp = x_prompt.reshape(bp * tp, d_model)
    xs = jnp.pad(x_sample, ((0, 0), (0, rows - ts), (0, 0))).reshape(bs * rows, d_model)
    mem = mem_prompt.reshape(bp * mem_len, d_model)

    cos_p, sin_p = _rope_tables(jnp.arange(tp, dtype=jnp.int32))
    cos_s, sin_s = _rope_tables(past_len + jnp.arange(rows, dtype=jnp.int32))
    kc = cache_moba_k.reshape(cache_moba_k.shape[0], cache_moba_k.shape[1], PAGE_SIZE * n_heads, HEAD_DIM)
    vc = cache_moba_v.reshape(kc.shape)
    mem_kc = cache_mem_k.reshape(depth, bs, mem_len, d_cross)
    mem_vc = cache_mem_v.reshape(depth, bs, mem_len, d_cross)

    tm = 512
    outs = {k: [] for k in ("moba_k_p", "moba_v_p", "hgrn_p", "memk_p", "memv_p", "moba_k_s", "moba_v_s", "hgrn_s")}
    for l in range(depth):
        j = l // N_MIXERS
        proj_p = _project(_rmsnorm(xp, gains, l, BF16, 256), w_in, l, tm, 512)
        proj_s = _project(_rmsnorm(xs, gains, l, BF16, bs * rows), w_in, l, bs * rows, 512)
        if l % N_MIXERS == 0:
            mix_p, st_p = _hgrn(proj_p, hgrn_lb_logits, out_gains, j, bp, n_heads, d_main, gate_off, None, GLA_CHUNK)
            mix_s, st_s = _hgrn(proj_s, hgrn_lb_logits, out_gains, j, bs, n_heads, d_main, gate_off, state_hgrn, ts)
            outs["hgrn_p"].append(st_p)
            outs["hgrn_s"].append(st_s)
        else:
            k_p, mix_p = _moba_prompt(proj_p, cos_p, sin_p, bp, n_heads, d_main, gate_off)
            mix_s, k_s = _moba_decode(proj_s, kc, vc, page_table, j, cos_s, sin_s, bs, n_heads, d_main, gate_off, ts)
            outs["moba_k_p"].append(k_p.reshape(bp, tp, n_heads, HEAD_DIM))
            outs["moba_v_p"].append(proj_p[:, 2 * d_main:3 * d_main].reshape(bp, tp, n_heads, HEAD_DIM))
            outs["moba_k_s"].append(k_s.reshape(bs, rows, n_heads, HEAD_DIM)[:, :ts])
            outs["moba_v_s"].append(proj_s[:, 2 * d_main:3 * d_main].reshape(bs, rows, n_heads, HEAD_DIM)[:, :ts])
        kv = _project(_rmsnorm(mem, mem_gains, l, BF16, 256), w_mem_kv, l, tm, 512)
        outs["memk_p"].append(kv[:, :d_cross].reshape(bp, mem_len, N_MEM_HEADS, mem_hd))
        outs["memv_p"].append(kv[:, d_cross:].reshape(bp, mem_len, N_MEM_HEADS, mem_hd))
        cross_p = _cross_attention(
            proj_p, kv, kv,
            pl.BlockSpec((mem_len, mem_hd), lambda b, h, i: (b, h)),
            pl.BlockSpec((mem_len, mem_hd), lambda b, h, i: (b, N_MEM_HEADS + h)),
            bp, 512, q_off, gate_off + d_main, d_cross)
        cached = pl.BlockSpec((None, None, mem_len, mem_hd), functools.partial(lambda b, h, i, l: (l, b, 0, h), l=l))
        cross_s = _cross_attention(proj_s, mem_kc, mem_vc, cached, cached, bs, rows, q_off, gate_off + d_main, d_cross)
        xp = _out_project(mix_p, cross_p, w_out, l, xp, tm, 512)
        xs = _out_project(mix_s, cross_s, w_out, l, xs, bs * rows, 512)

    final_g = final_norm_g.reshape(1, 1, d_model)
    y_prompt = _rmsnorm(xp, final_g, 0, F32, 256).reshape(bp, tp, d_model)
    y_sample = _rmsnorm(xs, final_g, 0, F32, bs * rows).reshape(bs, rows, d_model)[:, :ts]
    return (y_prompt, y_sample,
            jnp.stack(outs["moba_k_p"]), jnp.stack(outs["moba_v_p"]), jnp.stack(outs["hgrn_p"]),
            jnp.stack(outs["memk_p"]), jnp.stack(outs["memv_p"]),
            jnp.stack(outs["moba_k_s"]), jnp.stack(outs["moba_v_s"]), jnp.stack(outs["hgrn_s"]))
```

```python
import functools

import jax
import jax.numpy as jnp
from jax import lax
from jax.experimental import pallas as pl
from jax.experimental.pallas import tpu as pltpu

F32 = jnp.float32
BF16 = jnp.bfloat16

HEAD_DIM = 128
N_MEM_HEADS = 4
PAGE_SIZE = 128
MOBA_BLOCK = 256
MOBA_TOPK = 3
N_MIXERS = 2
ROPE_THETA = 10000.0
RMS_EPS = 1e-6
MIN_FORGET = 1e-30

LANES = 128
SUBLANES = 8
VMEM_LIMIT_BYTES = 52 * 1024 * 1024

GLA_CHUNK = 128
SAMPLE_ROWS = SUBLANES


def _params(*semantics):
    return pltpu.CompilerParams(dimension_semantics=semantics, vmem_limit_bytes=VMEM_LIMIT_BYTES)


def _nt_dot(a, b):
    return lax.dot_general(a.astype(BF16), b.astype(BF16), (((1,), (1,)), ((), ())),
                           preferred_element_type=F32)


def _nn_dot(a, b):
    return jnp.dot(a.astype(BF16), b.astype(BF16), preferred_element_type=F32)


def _silu(x):
    return x * jax.nn.sigmoid(x)


def _rmsnorm_body(x_ref, g_ref, o_ref):
    x = x_ref[...]
    inv = lax.rsqrt(jnp.mean(x * x, axis=-1, keepdims=True) + RMS_EPS)
    o_ref[...] = (x * inv * g_ref[...]).astype(o_ref.dtype)


def _rmsnorm(x, gains, layer, out_dtype, tm):
    m, d = x.shape
    return pl.pallas_call(
        _rmsnorm_body,
        grid=(m // tm,),
        in_specs=[pl.BlockSpec((tm, d), lambda i: (i, 0)),
                  pl.BlockSpec((None, 1, d), lambda i: (layer, 0, 0))],
        out_specs=pl.BlockSpec((tm, d), lambda i: (i, 0)),
        out_shape=jax.ShapeDtypeStruct((m, d), out_dtype),
        compiler_params=_params("arbitrary"),
        name="rmsnorm",
    )(x, gains)


def _proj_body(x_ref, w_ref, o_ref, wb_ref):
    @pl.when(pl.program_id(1) == 0)
    def _():
        wb_ref[...] = w_ref[...].astype(BF16)

    o_ref[...] = jnp.dot(x_ref[...], wb_ref[...], preferred_element_type=F32)


def _project(x, w, layer, tm, tn):
    m, kdim = x.shape
    n = w.shape[-1]
    return pl.pallas_call(
        _proj_body,
        grid=(n // tn, m // tm),
        in_specs=[pl.BlockSpec((tm, kdim), lambda j, i: (i, 0)),
                  pl.BlockSpec((None, kdim, tn), lambda j, i: (layer, 0, j))],
        out_specs=pl.BlockSpec((tm, tn), lambda j, i: (i, j)),
        out_shape=jax.ShapeDtypeStruct((m, n), F32),
        scratch_shapes=[pltpu.VMEM((kdim, tn), BF16)],
        compiler_params=_params("arbitrary", "arbitrary"),
        name="in_proj",
    )(x, w)


def _outproj_body(a_ref, c_ref, w_ref, r_ref, o_ref, wb_ref, *, d_main):
    @pl.when(pl.program_id(1) == 0)
    def _():
        wb_ref[...] = w_ref[...].astype(BF16)

    acc = jnp.dot(a_ref[...], wb_ref[:d_main, :], preferred_element_type=F32)
    acc = acc + jnp.dot(c_ref[...], wb_ref[d_main:, :], preferred_element_type=F32)
    o_ref[...] = r_ref[...] + acc


def _out_project(mix_main, mix_cross, w, layer, resid, tm, tn):
    m, d_main = mix_main.shape
    d_cross = mix_cross.shape[1]
    kdim, n = w.shape[1], w.shape[2]
    return pl.pallas_call(
        functools.partial(_outproj_body, d_main=d_main),
        grid=(n // tn, m // tm),
        in_specs=[pl.BlockSpec((tm, d_main), lambda j, i: (i, 0)),
                  pl.BlockSpec((tm, d_cross), lambda j, i: (i, 0)),
                  pl.BlockSpec((None, kdim, tn), lambda j, i: (layer, 0, j)),
                  pl.BlockSpec((tm, tn), lambda j, i: (i, j))],
        out_specs=pl.BlockSpec((tm, tn), lambda j, i: (i, j)),
        out_shape=jax.ShapeDtypeStruct((m, n), F32),
        scratch_shapes=[pltpu.VMEM((kdim, tn), BF16)],
        compiler_params=_params("arbitrary", "arbitrary"),
        name="out_proj",
    )(mix_main, mix_cross, w, resid)


def _hgrn_body(q_ref, f_ref, i_ref, g_ref, lbl_ref, og_ref, *rest, layer_j, n_valid, has_s0):
    if has_s0:
        s0_ref, o_ref, sfin_ref, mask_s = rest
    else:
        o_ref, sfin_ref, mask_s = rest
    c = GLA_CHUNK
    n_rows = q_ref.shape[0]

    n_layers = lbl_ref.shape[0]
    logits = [lbl_ref[i:i + 1, :] for i in range(n_layers)]
    mx = functools.reduce(jnp.maximum, logits)
    es = [jnp.exp(l - mx) for l in logits]
    tot = functools.reduce(lambda a, b: a + b, es)
    ws = [e / tot for e in es]
    csum = ws[0]
    for i in range(1, layer_j + 1):
        csum = csum + ws[i]
    lb = csum - ws[0]
    og = og_ref[...]

    row = lax.broadcasted_iota(jnp.int32, (c, HEAD_DIM), 0)
    tt = lax.broadcasted_iota(jnp.int32, (c, c), 0)
    ss = lax.broadcasted_iota(jnp.int32, (c, c), 1)
    n_levels = c.bit_length() - 1
    for lvl in range(n_levels):
        m = 1 << lvl
        pair = ((tt >> (lvl + 1)) == (ss >> (lvl + 1))) & ((tt & m) != 0) & ((ss & m) == 0)
        mask_s[lvl] = jnp.where(pair, 1.0, 0.0)
    mask_s[n_levels] = jnp.where(tt == ss, 1.0, 0.0)
    mask_s[n_levels + 1] = jnp.where(tt >= ss, 1.0, 0.0)

    def chunk(qc, fxc, vc, st):
        f = lb + (1.0 - lb) * jax.nn.sigmoid(fxc)
        logf = jnp.log(jnp.maximum(f, MIN_FORGET))
        kk = (1.0 - lb) * jax.nn.sigmoid(-fxc)
        if n_valid < c:
            ok = row < n_valid
            logf = jnp.where(ok, logf, 0.0)
            kk = jnp.where(ok, kk, 0.0)
        cum = logf
        s = 1
        while s < c:
            cum = cum + jnp.where(row >= s, pltpu.roll(cum, s, 0), 0.0)
            s *= 2
        att = _nt_dot(qc, kk) * mask_s[n_levels]
        end = cum
        for lvl in range(n_levels):
            m = 1 << lvl
            if 2 * m >= SUBLANES:
                g = c // (2 * m)
                ref = jnp.broadcast_to(cum.reshape(g, 2 * m, HEAD_DIM)[:, m - 1:m, :],
                                       (g, 2 * m, HEAD_DIM)).reshape(c, HEAD_DIM)
            else:
                upper = (row & m) != 0
                ref = jnp.where(upper, pltpu.roll(end, m, 0), end)
                end = jnp.where(upper, end, pltpu.roll(end, c - m, 0))
            e = jnp.exp(-jnp.abs(cum - ref))
            att = att + _nt_dot(qc * e, kk * e) * mask_s[lvl]
        inter = _nt_dot(qc * jnp.exp(cum), st)
        o = inter + _nn_dot(att, vc)
        last = cum[c - 1:c, :]
        kd = kk * jnp.exp(last - cum)
        st_new = st * jnp.exp(last) + _nn_dot(vc.T, kd)
        y = o * lax.rsqrt(jnp.mean(o * o, axis=-1, keepdims=True) + RMS_EPS) * og
        return y, st_new

    st0 = s0_ref[...].T if has_s0 else jnp.zeros((HEAD_DIM, HEAD_DIM), F32)
    if n_rows < c:
        pad = jnp.zeros((c - n_rows, HEAD_DIM), F32)
        grow = lambda ref: jnp.concatenate([ref[...], pad], axis=0)
        y, st = chunk(grow(q_ref), grow(f_ref), grow(i_ref), st0)
        o_ref[...] = (y[:n_rows] * _silu(g_ref[...])).astype(o_ref.dtype)
    else:
        def body(ci, st):
            r0 = pl.multiple_of(ci * c, c)
            rows = pl.ds(r0, c)
            y, st = chunk(q_ref[rows, :], f_ref[rows, :], i_ref[rows, :], st)
            o_ref[rows, :] = (y * _silu(g_ref[rows, :])).astype(o_ref.dtype)
            return st
        st = lax.fori_loop(0, n_rows // c, body, st0, unroll=2)
    sfin_ref[...] = st.T


def _hgrn(proj, lb_logits, out_g, layer_j, n_seq, n_heads, d_main, gate_off, s0, n_valid):
    t = proj.shape[0] // n_seq
    hb = d_main // HEAD_DIM
    col = lambda off: (lambda b, h: (b, off + h))
    in_specs = [pl.BlockSpec((t, HEAD_DIM), col(0)),
                pl.BlockSpec((t, HEAD_DIM), col(hb)),
                pl.BlockSpec((t, HEAD_DIM), col(2 * hb)),
                pl.BlockSpec((t, HEAD_DIM), col(gate_off // HEAD_DIM)),
                pl.BlockSpec((lb_logits.shape[0], HEAD_DIM), lambda b, h: (0, h)),
                pl.BlockSpec((None, 1, HEAD_DIM), lambda b, h: (layer_j, 0, 0))]
    args = [proj, proj, proj, proj, lb_logits, out_g]
    if s0 is not None:
        in_specs.append(pl.BlockSpec((None, None, None, HEAD_DIM, HEAD_DIM),
                                     lambda b, h: (layer_j, b, h, 0, 0)))
        args.append(s0)
    return pl.pallas_call(
        functools.partial(_hgrn_body, layer_j=layer_j, n_valid=n_valid, has_s0=s0 is not None),
        grid=(n_seq, n_heads),
        in_specs=in_specs,
        out_specs=[pl.BlockSpec((t, HEAD_DIM), lambda b, h: (b, h)),
                   pl.BlockSpec((None, None, HEAD_DIM, HEAD_DIM), lambda b, h: (b, h, 0, 0))],
        out_shape=[jax.ShapeDtypeStruct((n_seq * t, d_main), BF16),
                   jax.ShapeDtypeStruct((n_seq, n_heads, HEAD_DIM, HEAD_DIM), F32)],
        scratch_shapes=[pltpu.VMEM((GLA_CHUNK.bit_length() + 1, GLA_CHUNK, GLA_CHUNK), F32)],
        compiler_params=_params("arbitrary", "arbitrary"),
        name="hgrn2",
    )(*args)


def _rope_tables(pos):
    half = HEAD_DIM // 2
    inv = ROPE_THETA ** (-jnp.arange(half, dtype=F32) / half)
    ang = pos.astype(F32)[:, None] * inv[None, :]
    cos, sin = jnp.cos(ang), jnp.sin(ang)
    return jnp.concatenate([cos, cos], axis=-1), jnp.concatenate([-sin, sin], axis=-1)


def _rope(x, cos, sin):
    return x * cos + pltpu.roll(x, HEAD_DIM // 2, 1) * sin


def _moba_body(q_ref, k_ref, v_ref, g_ref, cos_ref, sin_ref, kout_ref, o_ref,
               qf_s, qb_s, kb_s, vt_s, km_s, s_s, p_s):
    blk = MOBA_BLOCK
    nblk = q_ref.shape[0] // blk
    scale = HEAD_DIM ** -0.5

    km_s[...] = jnp.zeros_like(km_s)
    for j in range(nblk):
        rows = pl.ds(j * blk, blk)
        cos, sin = cos_ref[rows, :], sin_ref[rows, :]
        q = _rope(q_ref[rows, :], cos, sin)
        k = _rope(k_ref[rows, :], cos, sin)
        kout_ref[rows, :] = k
        qf_s[j] = q
        qb_s[j] = q.astype(BF16)
        kb_s[rows, :] = k.astype(BF16)
        vt_s[:, rows] = v_ref[rows, :].T.astype(BF16)
        km_s[j:j + 1, :] = jnp.sum(k, axis=0, keepdims=True) * (1.0 / blk)

    npad = km_s.shape[0]
    jidx = lax.broadcasted_iota(jnp.int32, (npad, blk), 0)
    causal = (lax.broadcasted_iota(jnp.int32, (blk, blk), 0) <=
              lax.broadcasted_iota(jnp.int32, (blk, blk), 1))

    for i in range(nblk):
        n = (i + 1) * blk
        gate = lax.dot_general(km_s[...], qf_s[i], (((1,), (1,)), ((), ())),
                               precision=lax.Precision.HIGHEST, preferred_element_type=F32)
        cand = jidx < i
        gm = jnp.where(cand, gate, -jnp.inf)
        rank = jnp.zeros((npad, blk), jnp.int32)
        for jp in range(i):
            gj = gm[jp:jp + 1, :]
            rank = rank + jnp.where((gj > gm) | ((gj == gm) & (jp < jidx)), 1, 0)
        sel = jnp.where(cand & (rank < MOBA_TOPK) & (jnp.abs(gate) < jnp.inf), 1.0, 0.0)

        s_s[:n, :] = _nt_dot(kb_s[:n, :], qb_s[i])
        m = jnp.full((1, blk), -jnp.inf, F32)
        for j in range(i + 1):
            rows = pl.ds(j * blk, blk)
            mask = causal if j == i else jnp.broadcast_to(sel[j:j + 1, :], (blk, blk)) > 0.0
            st = jnp.where(mask, s_s[rows, :] * scale, -jnp.inf)
            s_s[rows, :] = st
            m = jnp.maximum(m, jnp.max(st, axis=0, keepdims=True))
        l = jnp.zeros((1, blk), F32)
        for j in range(i + 1):
            rows = pl.ds(j * blk, blk)
            p = jnp.exp(s_s[rows, :] - m)
            l = l + jnp.sum(p, axis=0, keepdims=True)
            p_s[rows, :] = p.astype(BF16)
        acc = jnp.dot(vt_s[:, :n], p_s[:n, :], preferred_element_type=F32)
        out = (acc / l).T
        rows = pl.ds(i * blk, blk)
        o_ref[rows, :] = (out * _silu(g_ref[rows, :])).astype(o_ref.dtype)


def _moba_prompt(proj, cos, sin, n_seq, n_heads, d_main, gate_off):
    t = proj.shape[0] // n_seq
    assert t % MOBA_BLOCK == 0
    nblk = t // MOBA_BLOCK
    hb = d_main // HEAD_DIM
    col = lambda off: (lambda b, h: (b, off + h))
    tab = pl.BlockSpec((t, HEAD_DIM), lambda b, h: (0, 0))
    return pl.pallas_call(
        _moba_body,
        grid=(n_seq, n_heads),
        in_specs=[pl.BlockSpec((t, HEAD_DIM), col(0)),
                  pl.BlockSpec((t, HEAD_DIM), col(hb)),
                  pl.BlockSpec((t, HEAD_DIM), col(2 * hb)),
                  pl.BlockSpec((t, HEAD_DIM), col(gate_off // HEAD_DIM)),
                  tab, tab],
        out_specs=[pl.BlockSpec((t, HEAD_DIM), lambda b, h: (b, h)),
                   pl.BlockSpec((t, HEAD_DIM), lambda b, h: (b, h))],
        out_shape=[jax.ShapeDtypeStruct((n_seq * t, d_main), F32),
                   jax.ShapeDtypeStruct((n_seq * t, d_main), BF16)],
        scratch_shapes=[pltpu.VMEM((nblk, MOBA_BLOCK, HEAD_DIM), F32),
                        pltpu.VMEM((nblk, MOBA_BLOCK, HEAD_DIM), BF16),
                        pltpu.VMEM((t, HEAD_DIM), BF16),
                        pltpu.VMEM((HEAD_DIM, t), BF16),
                        pltpu.VMEM((-(-nblk // SUBLANES) * SUBLANES, HEAD_DIM), F32),
                        pltpu.VMEM((t, MOBA_BLOCK), F32),
                        pltpu.VMEM((t, MOBA_BLOCK), BF16)],
        compiler_params=_params("arbitrary", "arbitrary"),
        name="moba_prompt",
    )(proj, proj, proj, proj, cos, sin)


def _moba_parts_body(pt_ref, k0_ref, k1_ref, v0_ref, v1_ref, q_ref, kn_ref, vn_ref, cos_ref, sin_ref,
                     po_ref, pm_ref, pl_ref, pg_ref, knew_ref, s_s, qk_s, *, n_heads, n_new, n_past_blocks):
    del pt_ref
    j = pl.program_id(1)
    scale = HEAD_DIM ** -0.5
    cos, sin = cos_ref[...], sin_ref[...]
    rows = q_ref.shape[0]

    def put(ref, h, val):
        ref[:, h * HEAD_DIM:(h + 1) * HEAD_DIM] = jnp.broadcast_to(val, (rows, HEAD_DIM))

    @pl.when(j < n_past_blocks)
    def _():
        for h in range(n_heads):
            hs = slice(h * HEAD_DIM, (h + 1) * HEAD_DIM)
            hr = slice(h * rows, (h + 1) * rows)
            head_rows = pl.ds(h, PAGE_SIZE, stride=n_heads)
            kh = jnp.concatenate([k0_ref[head_rows, :], k1_ref[head_rows, :]], axis=0)
            qh = _rope(q_ref[:, hs], cos, sin)
            s_s[hr, :] = _nt_dot(qh, kh) * scale
            qk_s[hr, :] = qh * (jnp.sum(kh, axis=0, keepdims=True) * (1.0 / MOBA_BLOCK))
        s = s_s[...]
        m = jnp.max(s, axis=1, keepdims=True)
        p = jnp.exp(s - m)
        s_s[...] = p
        lsum = jnp.sum(p, axis=1, keepdims=True)
        gate = jnp.sum(qk_s[...], axis=1, keepdims=True)
        for h in range(n_heads):
            hs = slice(h * HEAD_DIM, (h + 1) * HEAD_DIM)
            hr = slice(h * rows, (h + 1) * rows)
            head_rows = pl.ds(h, PAGE_SIZE, stride=n_heads)
            vh = jnp.concatenate([v0_ref[head_rows, :], v1_ref[head_rows, :]], axis=0)
            po_ref[:, hs] = _nn_dot(s_s[hr, :], vh)
            put(pm_ref, h, m[hr])
            put(pl_ref, h, lsum[hr])
            put(pg_ref, h, gate[hr])

    @pl.when(j == n_past_blocks)
    def _():
        t_i = lax.broadcasted_iota(jnp.int32, (rows, 1), 0)
        for h in range(n_heads):
            hs = slice(h * HEAD_DIM, (h + 1) * HEAD_DIM)
            qh = _rope(q_ref[:, hs], cos, sin)
            kh = _rope(kn_ref[:, hs], cos, sin)
            vh = vn_ref[:, hs]
            knew_ref[:, hs] = kh
            logit = [jnp.where(t_i >= s, jnp.sum(qh * kh[s:s + 1, :], axis=1, keepdims=True) * scale, -jnp.inf)
                     for s in range(n_new)]
            m = functools.reduce(jnp.maximum, logit)
            ps = [jnp.exp(lg - m) for lg in logit]
            po_ref[:, hs] = functools.reduce(lambda a, b: a + b, [p * vh[s:s + 1, :] for s, p in enumerate(ps)])
            put(pm_ref, h, m)
            put(pl_ref, h, functools.reduce(lambda a, b: a + b, ps))
            put(pg_ref, h, jnp.zeros((rows, 1), F32))


def _moba_combine_body(po_ref, pm_ref, pl_ref, pg_ref, g_ref, o_ref, cur_s, sel_s, *, n_past_blocks):
    nb = n_past_blocks
    shape = cur_s.shape
    jidx = lax.broadcasted_iota(jnp.int32, shape, 0)
    cur_s[...] = pg_ref[:nb]
    sel_s[...] = jnp.zeros(shape, F32)
    for _ in range(MOBA_TOPK):
        cur = cur_s[...]
        best = jnp.max(cur, axis=0, keepdims=True)
        first = jnp.min(jnp.where(cur == best, jidx, nb), axis=0, keepdims=True)
        pick = jidx == first
        sel_s[...] = jnp.where(pick & (jnp.abs(best) < jnp.inf), 1.0, sel_s[...])
        cur_s[...] = jnp.where(pick, -jnp.inf, cur)
    sel = sel_s[...] > 0.0
    m_own = pm_ref[nb]
    m_all = jnp.maximum(m_own, jnp.max(jnp.where(sel, pm_ref[:nb], -jnp.inf), axis=0))
    w = jnp.where(sel, jnp.exp(pm_ref[:nb] - m_all[None]), 0.0)
    w_own = jnp.exp(m_own - m_all)
    den = jnp.sum(w * pl_ref[:nb], axis=0) + w_own * pl_ref[nb]
    num = jnp.sum(w * po_ref[:nb], axis=0) + w_own * po_ref[nb]
    o_ref[...] = ((num / den) * _silu(g_ref[...])).astype(o_ref.dtype)


def _moba_decode(proj, k_cache, v_cache, page_table, layer_j, cos, sin, n_seq, n_heads, d_main, gate_off, n_new):
    rows = SAMPLE_ROWS
    n_pages = page_table.shape[1]
    pages_per_block = MOBA_BLOCK // PAGE_SIZE
    assert pages_per_block == 2 and n_pages % pages_per_block == 0
    nb = n_pages // pages_per_block
    page_rows = PAGE_SIZE * n_heads

    def page(p):
        def index(b, j, pt):
            return (layer_j, pt[b, jnp.minimum(pages_per_block * j + p, n_pages - 1)], 0, 0)
        return pl.BlockSpec((None, None, page_rows, HEAD_DIM), index)

    tok = lambda off: pl.BlockSpec((rows, d_main), lambda b, j, pt: (b, off))
    tab = pl.BlockSpec((rows, HEAD_DIM), lambda b, j, pt: (0, 0))
    part = pl.BlockSpec((None, None, rows, d_main), lambda b, j, pt: (b, j, 0, 0))
    part_shape = jax.ShapeDtypeStruct((n_seq, nb + 1, rows, d_main), F32)
    po, pm, pls, pg, k_new = pl.pallas_call(
        functools.partial(_moba_parts_body, n_heads=n_heads, n_new=n_new, n_past_blocks=nb),
        grid_spec=pltpu.PrefetchScalarGridSpec(
            num_scalar_prefetch=1,
            grid=(n_seq, nb + 1),
            in_specs=[page(0), page(1), page(0), page(1), tok(0), tok(1), tok(2), tab, tab],
            out_specs=[part, part, part, part,
                       pl.BlockSpec((rows, d_main), lambda b, j, pt: (b, 0))],
            scratch_shapes=[pltpu.VMEM((n_heads * rows, MOBA_BLOCK), F32),
                            pltpu.VMEM((n_heads * rows, HEAD_DIM), F32)]),
        out_shape=[part_shape, part_shape, part_shape, part_shape,
                   jax.ShapeDtypeStruct((n_seq * rows, d_main), F32)],
        compiler_params=_params("arbitrary", "arbitrary"),
        name="moba_decode_parts",
    )(page_table, k_cache, k_cache, v_cache, v_cache, proj, proj, proj, cos, sin)

    whole = pl.BlockSpec((None, nb + 1, rows, d_main), lambda b: (b, 0, 0, 0))
    gate_blk = 1024
    assert gate_off % gate_blk == 0 and d_main % gate_blk == 0
    n_gate = d_main // gate_blk
    gate_specs = [pl.BlockSpec((rows, gate_blk), functools.partial(lambda b, i: (b, gate_off // gate_blk + i), i=i))
                  for i in range(n_gate)]

    def body(po_ref, pm_ref, pl_ref, pg_ref, *rest):
        g_refs, (o_ref, cur_s, sel_s, g_s) = rest[:n_gate], rest[n_gate:]
        for i, g_ref in enumerate(g_refs):
            g_s[:, i * gate_blk:(i + 1) * gate_blk] = g_ref[...]
        _moba_combine_body(po_ref, pm_ref, pl_ref, pg_ref, g_s, o_ref, cur_s, sel_s, n_past_blocks=nb)

    mix = pl.pallas_call(
        body,
        grid=(n_seq,),
        in_specs=[whole, whole, whole, whole] + gate_specs,
        out_specs=pl.BlockSpec((rows, d_main), lambda b: (b, 0)),
        out_shape=jax.ShapeDtypeStruct((n_seq * rows, d_main), BF16),
        scratch_shapes=[pltpu.VMEM((nb, rows, d_main), F32), pltpu.VMEM((nb, rows, d_main), F32),
                        pltpu.VMEM((rows, d_main), F32)],
        compiler_params=_params("arbitrary"),
        name="moba_decode_combine",
    )(po, pm, pls, pg, *([proj] * n_gate))
    return mix, k_new


def _xattn_body(q_ref, k_ref, v_ref, g_ref, o_ref):
    scale = q_ref.shape[-1] ** -0.5
    s = _nt_dot(q_ref[...], k_ref[...]) * scale
    m = jnp.max(s, axis=-1, keepdims=True)
    p = jnp.exp(s - m)
    p = p / jnp.sum(p, axis=-1, keepdims=True)
    o_ref[...] = (_nn_dot(p, v_ref[...]) * _silu(g_ref[...])).astype(o_ref.dtype)


def _cross_attention(proj, mem_k, mem_v, k_spec, v_spec, n_seq, tq, q_off, gate_off, d_cross):
    t = proj.shape[0] // n_seq
    hd = d_cross // N_MEM_HEADS
    nq = t // tq
    row = lambda off: (lambda b, h, i: (b * nq + i, off + h))
    return pl.pallas_call(
        _xattn_body,
        grid=(n_seq, N_MEM_HEADS, nq),
        in_specs=[pl.BlockSpec((tq, hd), row(q_off // hd)),
                  k_spec, v_spec,
                  pl.BlockSpec((tq, hd), row(gate_off // hd))],
        out_specs=pl.BlockSpec((tq, hd), row(0)),
        out_shape=jax.ShapeDtypeStruct((n_seq * t, d_cross), BF16),
        compiler_params=_params("arbitrary", "arbitrary", "arbitrary"),
        name="mem_xattn",
    )(proj, mem_k, mem_v, proj)


def kernel(x_prompt, x_sample, cache_moba_k, cache_moba_v, state_hgrn, cache_mem_k, cache_mem_v,
           page_table, mem_prompt, norm_g, w_in, w_out, hgrn_lb_logits, hgrn_out_g, mem_norm_g,
           w_mem_kv, final_norm_g):
    bp, tp, d_model = x_prompt.shape
    bs, ts, _ = x_sample.shape
    depth = w_in.shape[0]
    d_mix = w_out.shape[1]
    d_cross = w_mem_kv.shape[2] // 2
    d_main = d_mix - d_cross
    n_heads = d_main // HEAD_DIM
    mem_len = mem_prompt.shape[1]
    mem_hd = d_cross // N_MEM_HEADS
    n_pages = page_table.shape[1]
    past_len = n_pages * PAGE_SIZE
    q_off = 3 * d_main
    gate_off = 3 * d_main + d_cross
    rows = SAMPLE_ROWS
    assert ts <= rows

    gains = norm_g.reshape(depth, 1, d_model)
    mem_gains = mem_norm_g.reshape(depth, 1, d_model)
    out_gains = hgrn_out_g.reshape(-1, 1, HEAD_DIM)

    xp = x_prompt.reshape(bp * tp, d_model)
    xs = jnp.pad(x_sample, ((0, 0), (0, rows - ts), (0, 0))).reshape(bs * rows, d_model)
    mem = mem_prompt.reshape(bp * mem_len, d_model)

    cos_p, sin_p = _rope_tables(jnp.arange(tp, dtype=jnp.int32))
    cos_s, sin_s = _rope_tables(past_len + jnp.arange(rows, dtype=jnp.int32))
    kc = cache_moba_k.reshape(cache_moba_k.shape[0], cache_moba_k.shape[1], PAGE_SIZE * n_heads, HEAD_DIM)
    vc = cache_moba_v.reshape(kc.shape)
    mem_kc = cache_mem_k.reshape(depth, bs, mem_len, d_cross)
    mem_vc = cache_mem_v.reshape(depth, bs, mem_len, d_cross)

    tm = 512
    outs = {k: [] for k in ("moba_k_p", "moba_v_p", "hgrn_p", "memk_p", "memv_p", "moba_k_s", "moba_v_s", "hgrn_s")}
    for l in range(depth):
        j = l // N_MIXERS
        proj_p = _project(_rmsnorm(xp, gains, l, BF16, 256), w_in, l, tm, 512)
        proj_s = _project(_rmsnorm(xs, gains, l, BF16, bs * rows), w_in, l, bs * rows, 512)
        if l % N_MIXERS == 0:
            mix_p, st_p = _hgrn(proj_p, hgrn_lb_logits, out_gains, j, bp, n_heads, d_main, gate_off, None, GLA_CHUNK)
            mix_s, st_s = _hgrn(proj_s, hgrn_lb_logits, out_gains, j, bs, n_heads, d_main, gate_off, state_hgrn, ts)
            outs["hgrn_p"].append(st_p)
            outs["hgrn_s"].append(st_s)
        else:
            k_p, mix_p = _moba_prompt(proj_p, cos_p, sin_p, bp, n_heads, d_main, gate_off)
            mix_s, k_s = _moba_decode(proj_s, kc, vc, page_table, j, cos_s, sin_s, bs, n_heads, d_main, gate_off, ts)
            outs["moba_k_p"].append(k_p.reshape(bp, tp, n_heads, HEAD_DIM))
            outs["moba_v_p"].append(proj_p[:, 2 * d_main:3 * d_main].reshape(bp, tp, n_heads, HEAD_DIM))
            outs["moba_k_s"].append(k_s.reshape(bs, rows, n_heads, HEAD_DIM)[:, :ts])
            outs["moba_v_s"].append(proj_s[:, 2 * d_main:3 * d_main].reshape(bs, rows, n_heads, HEAD_DIM)[:, :ts])
        kv = _project(_rmsnorm(mem, mem_gains, l, BF16, 256), w_mem_kv, l, tm, 512)
        outs["memk_p"].append(kv[:, :d_cross].reshape(bp, mem_len, N_MEM_HEADS, mem_hd))
        outs["memv_p"].append(kv[:, d_cross:].reshape(bp, mem_len, N_MEM_HEADS, mem_hd))
        cross_p = _cross_attention(
            proj_p, kv, kv,
            pl.BlockSpec((mem_len, mem_hd), lambda b, h, i: (b, h)),
            pl.BlockSpec((mem_len, mem_hd), lambda b, h, i: (b, N_MEM_HEADS + h)),
            bp, 512, q_off, gate_off + d_main, d_cross)
        cached = pl.BlockSpec((None, None, mem_len, mem_hd), functools.partial(lambda b, h, i, l: (l, b, 0, h), l=l))
        cross_s = _cross_attention(proj_s, mem_kc, mem_vc, cached, cached, bs, rows, q_off, gate_off + d_main, d_cross)
        xp = _out_project(mix_p, cross_p, w_out, l, xp, tm, 512)
        xs = _out_project(mix_s, cross_s, w_out, l, xs, bs * rows, 512)

    final_g = final_norm_g.reshape(1, 1, d_model)
    y_prompt = _rmsnorm(xp, final_g, 0, F32, 256).reshape(bp, tp, d_model)
    y_sample = _rmsnorm(xs, final_g, 0, F32, bs * rows).reshape(bs, rows, d_model)[:, :ts]
    return (y_prompt, y_sample,
            jnp.stack(outs["moba_k_p"]), jnp.stack(outs["moba_v_p"]), jnp.stack(outs["hgrn_p"]),
            jnp.stack(outs["memk_p"]), jnp.stack(outs["memv_p"]),
            jnp.stack(outs["moba_k_s"]), jnp.stack(outs["moba_v_s"]), jnp.stack(outs["hgrn_s"]))
```

```python
import functools

import jax
import jax.numpy as jnp
from jax import lax
from jax.experimental import pallas as pl
from jax.experimental.pallas import tpu as pltpu

F32 = jnp.float32
BF16 = jnp.bfloat16

HEAD_DIM = 128
N_MEM_HEADS = 4
PAGE_SIZE = 128
MOBA_BLOCK = 256
MOBA_TOPK = 3
N_MIXERS = 2
ROPE_THETA = 10000.0
RMS_EPS = 1e-6
MIN_FORGET = 1e-30
LOG2_E = 1.4426950408889634

LANES = 128
SUBLANES = 8
VMEM_LIMIT_BYTES = 52 * 1024 * 1024

PROJ_ROWS = 512
PROJ_COLS = 1024
OUT_PROJ_COLS = 1024
NORM_ROWS = 256
GLA_CHUNK = 128
SAMPLE_ROWS = SUBLANES


def _params(*semantics):
    return pltpu.CompilerParams(dimension_semantics=semantics, vmem_limit_bytes=VMEM_LIMIT_BYTES)


def _nt_dot(a, b):
    return lax.dot_general(a.astype(BF16), b.astype(BF16), (((1,), (1,)), ((), ())),
                           preferred_element_type=F32)


def _nn_dot(a, b):
    return jnp.dot(a.astype(BF16), b.astype(BF16), preferred_element_type=F32)


def _silu(x):
    return x * jax.nn.sigmoid(x)


def _rmsnorm_body(x_ref, g_ref, o_ref):
    x = x_ref[...]
    inv = lax.rsqrt(jnp.mean(x * x, axis=-1, keepdims=True) + RMS_EPS)
    o_ref[...] = (x * inv * g_ref[...]).astype(o_ref.dtype)


def _rmsnorm(x, gains, layer, out_dtype, tm):
    m, d = x.shape
    return pl.pallas_call(
        _rmsnorm_body,
        grid=(m // tm,),
        in_specs=[pl.BlockSpec((tm, d), lambda i: (i, 0)),
                  pl.BlockSpec((None, 1, d), lambda i: (layer, 0, 0))],
        out_specs=pl.BlockSpec((tm, d), lambda i: (i, 0)),
        out_shape=jax.ShapeDtypeStruct((m, d), out_dtype),
        compiler_params=_params("arbitrary"),
        name="rmsnorm",
    )(x, gains)


def _stage_weights(w_hbm, stage_ref, wb_ref, sem, layer, tn):
    j, i = pl.program_id(0), pl.program_id(1)

    def fetch(jj):
        cols = pl.ds(pl.multiple_of(jj * tn, tn), tn)
        return pltpu.make_async_copy(w_hbm.at[layer, :, cols], stage_ref, sem)

    @pl.when(i == 0)
    def _():
        @pl.when(j == 0)
        def _():
            fetch(0).start()

        fetch(j).wait()
        wb_ref[...] = stage_ref[...].astype(BF16)

        @pl.when(j + 1 < pl.num_programs(0))
        def _():
            fetch(j + 1).start()


def _weight_scratch(kdim, tn):
    return [pltpu.VMEM((kdim, tn), F32), pltpu.VMEM((kdim, tn), BF16), pltpu.SemaphoreType.DMA(())]


def _proj_body(x_ref, w_hbm, o_ref, stage_ref, wb_ref, sem, *, layer, tn):
    _stage_weights(w_hbm, stage_ref, wb_ref, sem, layer, tn)
    o_ref[...] = jnp.dot(x_ref[...], wb_ref[...], preferred_element_type=F32)


def _project(x, w, layer, tm, tn):
    m, kdim = x.shape
    n = w.shape[-1]
    return pl.pallas_call(
        functools.partial(_proj_body, layer=layer, tn=tn),
        grid=(n // tn, m // tm),
        in_specs=[pl.BlockSpec((tm, kdim), lambda j, i: (i, 0)),
                  pl.BlockSpec(memory_space=pl.ANY)],
        out_specs=pl.BlockSpec((tm, tn), lambda j, i: (i, j)),
        out_shape=jax.ShapeDtypeStruct((m, n), F32),
        scratch_shapes=_weight_scratch(kdim, tn),
        compiler_params=_params("arbitrary", "arbitrary"),
        name="in_proj",
    )(x, w)


def _outproj_body(a_ref, c_ref, w_hbm, r_ref, o_ref, stage_ref, wb_ref, sem, *, d_main, layer, tn):
    _stage_weights(w_hbm, stage_ref, wb_ref, sem, layer, tn)
    acc = jnp.dot(a_ref[...], wb_ref[:d_main, :], preferred_element_type=F32)
    acc = acc + jnp.dot(c_ref[...], wb_ref[d_main:, :], preferred_element_type=F32)
    o_ref[...] = r_ref[...] + acc


def _out_project(mix_main, mix_cross, w, layer, resid, tm, tn):
    m, d_main = mix_main.shape
    d_cross = mix_cross.shape[1]
    kdim, n = w.shape[1], w.shape[2]
    return pl.pallas_call(
        functools.partial(_outproj_body, d_main=d_main, layer=layer, tn=tn),
        grid=(n // tn, m // tm),
        in_specs=[pl.BlockSpec((tm, d_main), lambda j, i: (i, 0)),
                  pl.BlockSpec((tm, d_cross), lambda j, i: (i, 0)),
                  pl.BlockSpec(memory_space=pl.ANY),
                  pl.BlockSpec((tm, tn), lambda j, i: (i, j))],
        out_specs=pl.BlockSpec((tm, tn), lambda j, i: (i, j)),
        out_shape=jax.ShapeDtypeStruct((m, n), F32),
        scratch_shapes=_weight_scratch(kdim, tn),
        compiler_params=_params("arbitrary", "arbitrary"),
        name="out_proj",
    )(mix_main, mix_cross, w, resid)


def _hgrn_body(q_ref, f_ref, i_ref, g_ref, lbl_ref, og_ref, *rest, layer_j, n_valid, has_s0):
    rest = list(rest)
    s0_ref = rest.pop(0) if has_s0 else None
    o_ref, sfin_ref, mask_s = rest
    c = GLA_CHUNK
    n_rows = q_ref.shape[0]

    n_layers = lbl_ref.shape[0]
    logits = [lbl_ref[i:i + 1, :] for i in range(n_layers)]
    mx = functools.reduce(jnp.maximum, logits)
    es = [jnp.exp(l - mx) for l in logits]
    tot = functools.reduce(lambda a, b: a + b, es)
    ws = [e / tot for e in es]
    csum = ws[0]
    for i in range(1, layer_j + 1):
        csum = csum + ws[i]
    lb = csum - ws[0]
    og = og_ref[...]

    row = lax.broadcasted_iota(jnp.int32, (c, HEAD_DIM), 0)
    tt = lax.broadcasted_iota(jnp.int32, (c, c), 0)
    ss = lax.broadcasted_iota(jnp.int32, (c, c), 1)
    n_levels = c.bit_length() - 1
    for lvl in range(n_levels):
        m = 1 << lvl
        pair = ((tt >> (lvl + 1)) == (ss >> (lvl + 1))) & ((tt & m) != 0) & ((ss & m) == 0)
        mask_s[lvl] = jnp.where(pair, 1.0, 0.0)
    mask_s[n_levels] = jnp.where(tt == ss, 1.0, 0.0)

    def prefix_sum(x):
        s = 1
        while s < c:
            x = x + jnp.where(row >= s, pltpu.roll(x, s, 0), 0.0)
            s *= 2
        return x

    def gates_part(fxc):
        f = lb + (1.0 - lb) * jax.nn.sigmoid(fxc)
        logf = jnp.log(jnp.maximum(f, MIN_FORGET))
        kk = (1.0 - lb) * jax.nn.sigmoid(-fxc)
        if n_valid < c:
            ok = row < n_valid
            logf = jnp.where(ok, logf, 0.0)
            kk = jnp.where(ok, kk, 0.0)
        return kk, prefix_sum(logf)

    def local_part(qc, kk, cum, vc):
        att = _nt_dot(qc, kk) * mask_s[n_levels]
        end = cum
        for lvl in range(n_levels):
            m = 1 << lvl
            if 2 * m >= SUBLANES:
                g = c // (2 * m)
                ref = jnp.broadcast_to(cum.reshape(g, 2 * m, HEAD_DIM)[:, m - 1:m, :],
                                       (g, 2 * m, HEAD_DIM)).reshape(c, HEAD_DIM)
            else:
                upper = (row & m) != 0
                ref = jnp.where(upper, pltpu.roll(end, m, 0), end)
                end = jnp.where(upper, end, pltpu.roll(end, c - m, 0))
            e = jnp.exp2(jnp.abs(cum - ref) * (-LOG2_E))
            att = att + _nt_dot(qc * e, kk * e) * mask_s[lvl]
        last = cum[c - 1:c, :]
        intra = _nn_dot(att, vc)
        q_dec = (qc * jnp.exp(cum)).astype(BF16)
        kv = _nn_dot(vc.T, kk * jnp.exp(last - cum))
        return intra, q_dec, kv, jnp.exp(last)

    def state_part(intra, q_dec, kv, dec, st, gate):
        o = _nt_dot(q_dec, st) + intra
        y = o * lax.rsqrt(jnp.mean(o * o, axis=-1, keepdims=True) + RMS_EPS) * og
        return y * _silu(gate), st * dec + kv

    st0 = s0_ref[...].T if has_s0 else jnp.zeros((HEAD_DIM, HEAD_DIM), F32)
    if n_rows < c:
        pad = jnp.zeros((c - n_rows, HEAD_DIM), F32)
        grow = lambda ref: jnp.concatenate([ref[...], pad], axis=0)
        y, st = state_part(*local_part(grow(q_ref), *gates_part(grow(f_ref)), grow(i_ref)), st0, grow(g_ref))
        o_ref[...] = y[:n_rows].astype(o_ref.dtype)
    else:
        def body(ci, st):
            rows = pl.ds(pl.multiple_of(ci * c, c), c)
            local = local_part(q_ref[rows, :], *gates_part(f_ref[rows, :]), i_ref[rows, :])
            y, st = state_part(*local, st, g_ref[rows, :])
            o_ref[rows, :] = y.astype(o_ref.dtype)
            return st

        st = lax.fori_loop(0, n_rows // c, body, st0, unroll=2)
    sfin_ref[...] = st.T


def _hgrn(proj, lb_logits, out_g, layer_j, n_seq, n_heads, d_main, gate_off, s0, n_valid):
    t = proj.shape[0] // n_seq
    hb = d_main // HEAD_DIM
    col = lambda off: (lambda b, h: (b, off + h))
    in_specs = [pl.BlockSpec((t, HEAD_DIM), col(0)),
                pl.BlockSpec((t, HEAD_DIM), col(hb)),
                pl.BlockSpec((t, HEAD_DIM), col(2 * hb)),
                pl.BlockSpec((t, HEAD_DIM), col(gate_off // HEAD_DIM)),
                pl.BlockSpec((lb_logits.shape[0], HEAD_DIM), lambda b, h: (0, h)),
                pl.BlockSpec((None, 1, HEAD_DIM), lambda b, h: (layer_j, 0, 0))]
    args = [proj, proj, proj, proj, lb_logits, out_g]
    if s0 is not None:
        in_specs.append(pl.BlockSpec((None, None, None, HEAD_DIM, HEAD_DIM),
                                     lambda b, h: (layer_j, b, h, 0, 0)))
        args.append(s0)
    return pl.pallas_call(
        functools.partial(_hgrn_body, layer_j=layer_j, n_valid=n_valid, has_s0=s0 is not None),
        grid=(n_seq, n_heads),
        in_specs=in_specs,
        out_specs=[pl.BlockSpec((t, HEAD_DIM), lambda b, h: (b, h)),
                   pl.BlockSpec((None, None, HEAD_DIM, HEAD_DIM), lambda b, h: (b, h, 0, 0))],
        out_shape=[jax.ShapeDtypeStruct((n_seq * t, d_main), BF16),
                   jax.ShapeDtypeStruct((n_seq, n_heads, HEAD_DIM, HEAD_DIM), F32)],
        scratch_shapes=[pltpu.VMEM((GLA_CHUNK.bit_length(), GLA_CHUNK, GLA_CHUNK), F32)],
        compiler_params=_params("arbitrary", "arbitrary"),
        name="hgrn2",
    )(*args)


def _rope_tables(pos):
    half = HEAD_DIM // 2
    inv = ROPE_THETA ** (-jnp.arange(half, dtype=F32) / half)
    ang = pos.astype(F32)[:, None] * inv[None, :]
    cos, sin = jnp.cos(ang), jnp.sin(ang)
    return jnp.concatenate([cos, cos], axis=-1), jnp.concatenate([-sin, sin], axis=-1)


def _rope(x, cos, sin):
    return x * cos + pltpu.roll(x, HEAD_DIM // 2, 1) * sin


def _moba_body(q_ref, k_ref, v_ref, g_ref, cos_ref, sin_ref, kout_ref, o_ref,
               qf_s, qb_s, kb_s, vt_s, km_s, s_s, p_s):
    blk = MOBA_BLOCK
    nblk = q_ref.shape[0] // blk
    scale = HEAD_DIM ** -0.5

    km_s[...] = jnp.zeros_like(km_s)
    for j in range(nblk):
        rows = pl.ds(j * blk, blk)
        cos, sin = cos_ref[rows, :], sin_ref[rows, :]
        q = _rope(q_ref[rows, :], cos, sin)
        k = _rope(k_ref[rows, :], cos, sin)
        kout_ref[rows, :] = k
        qf_s[j] = q
        qb_s[j] = q.astype(BF16)
        kb_s[rows, :] = k.astype(BF16)
        vt_s[:HEAD_DIM, rows] = v_ref[rows, :].T.astype(BF16)
        vt_s[HEAD_DIM:, rows] = jnp.ones((vt_s.shape[0] - HEAD_DIM, blk), BF16)
        km_s[j:j + 1, :] = jnp.sum(k, axis=0, keepdims=True) * (1.0 / blk)

    npad = km_s.shape[0]
    jidx = lax.broadcasted_iota(jnp.int32, (npad, blk), 0)
    causal = (lax.broadcasted_iota(jnp.int32, (blk, blk), 0) <=
              lax.broadcasted_iota(jnp.int32, (blk, blk), 1))

    for i in range(nblk):
        n = (i + 1) * blk
        gate = lax.dot_general(km_s[...], qf_s[i], (((1,), (1,)), ((), ())),
                               precision=lax.Precision.HIGHEST, preferred_element_type=F32)
        cand = jidx < i
        gm = jnp.where(cand, gate, -jnp.inf)
        rank = jnp.zeros((npad, blk), jnp.int32)
        for jp in range(i):
            gj = gm[jp:jp + 1, :]
            rank = rank + jnp.where((gj > gm) | ((gj == gm) & (jp < jidx)), 1, 0)
        sel = jnp.where(cand & (rank < MOBA_TOPK) & (jnp.abs(gate) < jnp.inf), 1.0, 0.0)

        m = jnp.full((1, blk), -jnp.inf, F32)
        qi = qb_s[i]
        for j in range(i + 1):
            rows = pl.ds(j * blk, blk)
            mask = causal if j == i else jnp.broadcast_to(sel[j:j + 1, :], (blk, blk)) > 0.0
            st = jnp.where(mask, _nt_dot(kb_s[rows, :], qi), -jnp.inf)
            s_s[rows, :] = st
            m = jnp.maximum(m, jnp.max(st, axis=0, keepdims=True))
        for j in range(i + 1):
            rows = pl.ds(j * blk, blk)
            p_s[rows, :] = jnp.exp2((s_s[rows, :] - m) * (scale * LOG2_E)).astype(BF16)
        acc = jnp.dot(vt_s[:, :n], p_s[:n, :], preferred_element_type=F32)
        out = (acc[:HEAD_DIM] / acc[HEAD_DIM:HEAD_DIM + 1]).T
        rows = pl.ds(i * blk, blk)
        o_ref[rows, :] = (out * _silu(g_ref[rows, :])).astype(o_ref.dtype)


def _moba_prompt(proj, cos, sin, n_seq, n_heads, d_main, gate_off):
    t = proj.shape[0] // n_seq
    assert t % MOBA_BLOCK == 0
    nblk = t // MOBA_BLOCK
    hb = d_main // HEAD_DIM
    col = lambda off: (lambda b, h: (b, off + h))
    tab = pl.BlockSpec((t, HEAD_DIM), lambda b, h: (0, 0))
    return pl.pallas_call(
        _moba_body,
        grid=(n_seq, n_heads),
        in_specs=[pl.BlockSpec((t, HEAD_DIM), col(0)),
                  pl.BlockSpec((t, HEAD_DIM), col(hb)),
                  pl.BlockSpec((t, HEAD_DIM), col(2 * hb)),
                  pl.BlockSpec((t, HEAD_DIM), col(gate_off // HEAD_DIM)),
                  tab, tab],
        out_specs=[pl.BlockSpec((t, HEAD_DIM), lambda b, h: (b, h)),
                   pl.BlockSpec((t, HEAD_DIM), lambda b, h: (b, h))],
        out_shape=[jax.ShapeDtypeStruct((n_seq * t, d_main), F32),
                   jax.ShapeDtypeStruct((n_seq * t, d_main), BF16)],
        scratch_shapes=[pltpu.VMEM((nblk, MOBA_BLOCK, HEAD_DIM), F32),
                        pltpu.VMEM((nblk, MOBA_BLOCK, HEAD_DIM), BF16),
                        pltpu.VMEM((t, HEAD_DIM), BF16),
                        pltpu.VMEM((HEAD_DIM + 2 * SUBLANES, t), BF16),
                        pltpu.VMEM((-(-nblk // SUBLANES) * SUBLANES, HEAD_DIM), F32),
                        pltpu.VMEM((t, MOBA_BLOCK), F32),
                        pltpu.VMEM((t, MOBA_BLOCK), BF16)],
        compiler_params=_params("arbitrary", "arbitrary"),
        name="moba_prompt",
    )(proj, proj, proj, proj, cos, sin)


def _moba_parts_body(pt_ref, k0_ref, k1_ref, v0_ref, v1_ref, q_ref, kn_ref, vn_ref, cos_ref, sin_ref,
                     po_ref, pm_ref, pl_ref, pg_ref, knew_ref, s_s, qk_s, *, n_heads, n_new, n_past_blocks):
    del pt_ref
    j = pl.program_id(1)
    scale = HEAD_DIM ** -0.5
    cos, sin = cos_ref[...], sin_ref[...]
    rows = q_ref.shape[0]

    def put(ref, h, val):
        ref[:, h * HEAD_DIM:(h + 1) * HEAD_DIM] = jnp.broadcast_to(val, (rows, HEAD_DIM))

    @pl.when(j < n_past_blocks)
    def _():
        for h in range(n_heads):
            hs = slice(h * HEAD_DIM, (h + 1) * HEAD_DIM)
            hr = slice(h * rows, (h + 1) * rows)
            head_rows = pl.ds(h, PAGE_SIZE, stride=n_heads)
            kh = jnp.concatenate([k0_ref[head_rows, :], k1_ref[head_rows, :]], axis=0)
            qh = _rope(q_ref[:, hs], cos, sin)
            s_s[hr, :] = _nt_dot(qh, kh) * scale
            qk_s[hr, :] = qh * (jnp.sum(kh, axis=0, keepdims=True) * (1.0 / MOBA_BLOCK))
        s = s_s[...]
        m = jnp.max(s, axis=1, keepdims=True)
        p = jnp.exp(s - m)
        s_s[...] = p
        lsum = jnp.sum(p, axis=1, keepdims=True)
        gate = jnp.sum(qk_s[...], axis=1, keepdims=True)
        for h in range(n_heads):
            hs = slice(h * HEAD_DIM, (h + 1) * HEAD_DIM)
            hr = slice(h * rows, (h + 1) * rows)
            head_rows = pl.ds(h, PAGE_SIZE, stride=n_heads)
            vh = jnp.concatenate([v0_ref[head_rows, :], v1_ref[head_rows, :]], axis=0)
            po_ref[:, hs] = _nn_dot(s_s[hr, :], vh)
            put(pm_ref, h, m[hr])
            put(pl_ref, h, lsum[hr])
            put(pg_ref, h, gate[hr])

    @pl.when(j == n_past_blocks)
    def _():
        t_i = lax.broadcasted_iota(jnp.int32, (rows, 1), 0)
        for h in range(n_heads):
            hs = slice(h * HEAD_DIM, (h + 1) * HEAD_DIM)
            qh = _rope(q_ref[:, hs], cos, sin)
            kh = _rope(kn_ref[:, hs], cos, sin)
            vh = vn_ref[:, hs]
            knew_ref[:, hs] = kh
            logit = [jnp.where(t_i >= s, jnp.sum(qh * kh[s:s + 1, :], axis=1, keepdims=True) * scale, -jnp.inf)
                     for s in range(n_new)]
            m = functools.reduce(jnp.maximum, logit)
            ps = [jnp.exp(lg - m) for lg in logit]
            po_ref[:, hs] = functools.reduce(lambda a, b: a + b, [p * vh[s:s + 1, :] for s, p in enumerate(ps)])
            put(pm_ref, h, m)
            put(pl_ref, h, functools.reduce(lambda a, b: a + b, ps))
            put(pg_ref, h, jnp.zeros((rows, 1), F32))


def _moba_combine_body(po_ref, pm_ref, pl_ref, pg_ref, g_ref, o_ref, cur_s, sel_s, *, n_past_blocks):
    nb = n_past_blocks
    shape = cur_s.shape
    jidx = lax.broadcasted_iota(jnp.int32, shape, 0)
    cur_s[...] = pg_ref[:nb]
    sel_s[...] = jnp.zeros(shape, F32)
    for _ in range(MOBA_TOPK):
        cur = cur_s[...]
        best = jnp.max(cur, axis=0, keepdims=True)
        first = jnp.min(jnp.where(cur == best, jidx, nb), axis=0, keepdims=True)
        pick = jidx == first
        sel_s[...] = jnp.where(pick & (jnp.abs(best) < jnp.inf), 1.0, sel_s[...])
        cur_s[...] = jnp.where(pick, -jnp.inf, cur)
    sel = sel_s[...] > 0.0
    m_own = pm_ref[nb]
    m_all = jnp.maximum(m_own, jnp.max(jnp.where(sel, pm_ref[:nb], -jnp.inf), axis=0))
    w = jnp.where(sel, jnp.exp(pm_ref[:nb] - m_all[None]), 0.0)
    w_own = jnp.exp(m_own - m_all)
    den = jnp.sum(w * pl_ref[:nb], axis=0) + w_own * pl_ref[nb]
    num = jnp.sum(w * po_ref[:nb], axis=0) + w_own * po_ref[nb]
    o_ref[...] = ((num / den) * _silu(g_ref[...])).astype(o_ref.dtype)


def _moba_decode(proj, k_cache, v_cache, page_table, layer_j, cos, sin, n_seq, n_heads, d_main, gate_off, n_new):
    rows = SAMPLE_ROWS
    n_pages = page_table.shape[1]
    pages_per_block = MOBA_BLOCK // PAGE_SIZE
    assert pages_per_block == 2 and n_pages % pages_per_block == 0
    nb = n_pages // pages_per_block
    page_rows = PAGE_SIZE * n_heads

    def page(p):
        def index(b, j, pt):
            return (layer_j, pt[b, jnp.minimum(pages_per_block * j + p, n_pages - 1)], 0, 0)
        return pl.BlockSpec((None, None, page_rows, HEAD_DIM), index)

    tok = lambda off: pl.BlockSpec((rows, d_main), lambda b, j, pt: (b, off))
    tab = pl.BlockSpec((rows, HEAD_DIM), lambda b, j, pt: (0, 0))
    part = pl.BlockSpec((None, None, rows, d_main), lambda b, j, pt: (b, j, 0, 0))
    part_shape = jax.ShapeDtypeStruct((n_seq, nb + 1, rows, d_main), F32)
    po, pm, pls, pg, k_new = pl.pallas_call(
        functools.partial(_moba_parts_body, n_heads=n_heads, n_new=n_new, n_past_blocks=nb),
        grid_spec=pltpu.PrefetchScalarGridSpec(
            num_scalar_prefetch=1,
            grid=(n_seq, nb + 1),
            in_specs=[page(0), page(1), page(0), page(1), tok(0), tok(1), tok(2), tab, tab],
            out_specs=[part, part, part, part,
                       pl.BlockSpec((rows, d_main), lambda b, j, pt: (b, 0))],
            scratch_shapes=[pltpu.VMEM((n_heads * rows, MOBA_BLOCK), F32),
                            pltpu.VMEM((n_heads * rows, HEAD_DIM), F32)]),
        out_shape=[part_shape, part_shape, part_shape, part_shape,
                   jax.ShapeDtypeStruct((n_seq * rows, d_main), F32)],
        compiler_params=_params("arbitrary", "arbitrary"),
        name="moba_decode_parts",
    )(page_table, k_cache, k_cache, v_cache, v_cache, proj, proj, proj, cos, sin)

    whole = pl.BlockSpec((None, nb + 1, rows, d_main), lambda b: (b, 0, 0, 0))
    gate_blk = 1024
    assert gate_off % gate_blk == 0 and d_main % gate_blk == 0
    n_gate = d_main // gate_blk
    gate_specs = [pl.BlockSpec((rows, gate_blk), functools.partial(lambda b, i: (b, gate_off // gate_blk + i), i=i))
                  for i in range(n_gate)]

    def body(po_ref, pm_ref, pl_ref, pg_ref, *rest):
        g_refs, (o_ref, cur_s, sel_s, g_s) = rest[:n_gate], rest[n_gate:]
        for i, g_ref in enumerate(g_refs):
            g_s[:, i * gate_blk:(i + 1) * gate_blk] = g_ref[...]
        _moba_combine_body(po_ref, pm_ref, pl_ref, pg_ref, g_s, o_ref, cur_s, sel_s, n_past_blocks=nb)

    mix = pl.pallas_call(
        body,
        grid=(n_seq,),
        in_specs=[whole, whole, whole, whole] + gate_specs,
        out_specs=pl.BlockSpec((rows, d_main), lambda b: (b, 0)),
        out_shape=jax.ShapeDtypeStruct((n_seq * rows, d_main), BF16),
        scratch_shapes=[pltpu.VMEM((nb, rows, d_main), F32), pltpu.VMEM((nb, rows, d_main), F32),
                        pltpu.VMEM((rows, d_main), F32)],
        compiler_params=_params("arbitrary"),
        name="moba_decode_combine",
    )(po, pm, pls, pg, *([proj] * n_gate))
    return mix, k_new


def _xattn_body(q_ref, k_ref, v_ref, g_ref, o_ref):
    scale = q_ref.shape[-1] ** -0.5
    s = _nt_dot(q_ref[...], k_ref[...]) * scale
    m = jnp.max(s, axis=-1, keepdims=True)
    p = jnp.exp(s - m)
    p = p / jnp.sum(p, axis=-1, keepdims=True)
    o_ref[...] = (_nn_dot(p, v_ref[...]) * _silu(g_ref[...])).astype(o_ref.dtype)


def _cross_attention(proj, mem_k, mem_v, k_spec, v_spec, n_seq, tq, q_off, gate_off, d_cross):
    t = proj.shape[0] // n_seq
    hd = d_cross // N_MEM_HEADS
    nq = t // tq
    row = lambda off: (lambda b, h, i: (b * nq + i, off + h))
    return pl.pallas_call(
        _xattn_body,
        grid=(n_seq, N_MEM_HEADS, nq),
        in_specs=[pl.BlockSpec((tq, hd), row(q_off // hd)),
                  k_spec, v_spec,
                  pl.BlockSpec((tq, hd), row(gate_off // hd))],
        out_specs=pl.BlockSpec((tq, hd), row(0)),
        out_shape=jax.ShapeDtypeStruct((n_seq * t, d_cross), BF16),
        compiler_params=_params("arbitrary", "arbitrary", "arbitrary"),
        name="mem_xattn",
    )(proj, mem_k, mem_v, proj)


def kernel(x_prompt, x_sample, cache_moba_k, cache_moba_v, state_hgrn, cache_mem_k, cache_mem_v,
           page_table, mem_prompt, norm_g, w_in, w_out, hgrn_lb_logits, hgrn_out_g, mem_norm_g,
           w_mem_kv, final_norm_g):
    bp, tp, d_model = x_prompt.shape
    bs, ts, _ = x_sample.shape
    depth = w_in.shape[0]
    d_mix = w_out.shape[1]
    d_cross = w_mem_kv.shape[2] // 2
    d_main = d_mix - d_cross
    n_heads = d_main // HEAD_DIM
    mem_len = mem_prompt.shape[1]
    mem_hd = d_cross // N_MEM_HEADS
    n_pages = page_table.shape[1]
    past_len = n_pages * PAGE_SIZE
    q_off = 3 * d_main
    gate_off = 3 * d_main + d_cross
    rows = SAMPLE_ROWS
    assert ts <= rows

    gains = norm_g.reshape(depth, 1, d_model)
    mem_gains = mem_norm_g.reshape(depth, 1, d_model)
    out_gains = hgrn_out_g.reshape(-1, 1, HEAD_DIM)

    xp = x_prompt.reshape(bp * tp, d_model)
    xs = jnp.pad(x_sample, ((0, 0), (0, rows - ts), (0, 0))).reshape(bs * rows, d_model)
    mem = mem_prompt.reshape(bp * mem_len, d_model)

    cos_p, sin_p = _rope_tables(jnp.arange(tp, dtype=jnp.int32))
    cos_s, sin_s = _rope_tables(past_len + jnp.arange(rows, dtype=jnp.int32))
    kc = cache_moba_k.reshape(cache_moba_k.shape[0], cache_moba_k.shape[1], PAGE_SIZE * n_heads, HEAD_DIM)
    vc = cache_moba_v.reshape(kc.shape)
    mem_kc = cache_mem_k.reshape(depth, bs, mem_len, d_cross)
    mem_vc = cache_mem_v.reshape(depth, bs, mem_len, d_cross)

    tm = PROJ_ROWS
    outs = {k: [] for k in ("moba_k_p", "moba_v_p", "hgrn_p", "memk_p", "memv_p", "moba_k_s", "moba_v_s", "hgrn_s")}
    for l in range(depth):
        j = l // N_MIXERS
        proj_p = _project(_rmsnorm(xp, gains, l, BF16, NORM_ROWS), w_in, l, tm, PROJ_COLS)
        proj_s = _project(_rmsnorm(xs, gains, l, BF16, bs * rows), w_in, l, bs * rows, PROJ_COLS)
        if l % N_MIXERS == 0:
            mix_p, st_p = _hgrn(proj_p, hgrn_lb_logits, out_gains, j, bp, n_heads, d_main, gate_off, None, GLA_CHUNK)
            mix_s, st_s = _hgrn(proj_s, hgrn_lb_logits, out_gains, j, bs, n_heads, d_main, gate_off, state_hgrn, ts)
            outs["hgrn_p"].append(st_p)
            outs["hgrn_s"].append(st_s)
        else:
            k_p, mix_p = _moba_prompt(proj_p, cos_p, sin_p, bp, n_heads, d_main, gate_off)
            mix_s, k_s = _moba_decode(proj_s, kc, vc, page_table, j, cos_s, sin_s, bs, n_heads, d_main, gate_off, ts)
            outs["moba_k_p"].append(k_p.reshape(bp, tp, n_heads, HEAD_DIM))
            outs["moba_v_p"].append(proj_p[:, 2 * d_main:3 * d_main].reshape(bp, tp, n_heads, HEAD_DIM))
            outs["moba_k_s"].append(k_s.reshape(bs, rows, n_heads, HEAD_DIM)[:, :ts])
            outs["moba_v_s"].append(proj_s[:, 2 * d_main:3 * d_main].reshape(bs, rows, n_heads, HEAD_DIM)[:, :ts])
        kv = _project(_rmsnorm(mem, mem_gains, l, BF16, NORM_ROWS), w_mem_kv, l, tm, PROJ_COLS)
        outs["memk_p"].append(kv[:, :d_cross].reshape(bp, mem_len, N_MEM_HEADS, mem_hd))
        outs["memv_p"].append(kv[:, d_cross:].reshape(bp, mem_len, N_MEM_HEADS, mem_hd))
        cross_p = _cross_attention(
            proj_p, kv, kv,
            pl.BlockSpec((mem_len, mem_hd), lambda b, h, i: (b, h)),
            pl.BlockSpec((mem_len, mem_hd), lambda b, h, i: (b, N_MEM_HEADS + h)),
            bp, 512, q_off, gate_off + d_main, d_cross)
        cached = pl.BlockSpec((None, None, mem_len, mem_hd), functools.partial(lambda b, h, i, l: (l, b, 0, h), l=l))
        cross_s = _cross_attention(proj_s, mem_kc, mem_vc, cached, cached, bs, rows, q_off, gate_off + d_main, d_cross)
        xp = _out_project(mix_p, cross_p, w_out, l, xp, tm, OUT_PROJ_COLS)
        xs = _out_project(mix_s, cross_s, w_out, l, xs, bs * rows, OUT_PROJ_COLS)

    final_g = final_norm_g.reshape(1, 1, d_model)
    y_prompt = _rmsnorm(xp, final_g, 0, F32, NORM_ROWS).reshape(bp, tp, d_model)
    y_sample = _rmsnorm(xs, final_g, 0, F32, bs * rows).reshape(bs, rows, d_model)[:, :ts]
    return (y_prompt, y_sample,
            jnp.stack(outs["moba_k_p"]), jnp.stack(outs["moba_v_p"]), jnp.stack(outs["hgrn_p"]),
            jnp.stack(outs["memk_p"]), jnp.stack(outs["memv_p"]),
            jnp.stack(outs["moba_k_s"]), jnp.stack(outs["moba_v_s"]), jnp.stack(outs["hgrn_s"]))
```

```python
import functools

import jax
import jax.numpy as jnp
from jax import lax
from jax.experimental import pallas as pl
from jax.experimental.pallas import tpu as pltpu

F32 = jnp.float32
BF16 = jnp.bfloat16

HEAD_DIM = 128
N_MEM_HEADS = 4
PAGE_SIZE = 128
MOBA_BLOCK = 256
MOBA_TOPK = 3
N_MIXERS = 2
ROPE_THETA = 10000.0
RMS_EPS = 1e-6
MIN_FORGET = 1e-30
LOG2_E = 1.4426950408889634

LANES = 128
SUBLANES = 8
VMEM_LIMIT_BYTES = 52 * 1024 * 1024

PROJ_ROWS = 512
PROJ_COLS = 1024
OUT_PROJ_COLS = 1024
NORM_ROWS = 256
GLA_CHUNK = 128
SAMPLE_ROWS = SUBLANES


def _params(*semantics):
    return pltpu.CompilerParams(dimension_semantics=semantics, vmem_limit_bytes=VMEM_LIMIT_BYTES)


def _nt_dot(a, b):
    return lax.dot_general(a.astype(BF16), b.astype(BF16), (((1,), (1,)), ((), ())),
                           preferred_element_type=F32)


def _nn_dot(a, b):
    return jnp.dot(a.astype(BF16), b.astype(BF16), preferred_element_type=F32)


def _silu(x):
    return x * jax.nn.sigmoid(x)


def _rmsnorm_body(x_ref, g_ref, o_ref):
    x = x_ref[...]
    inv = lax.rsqrt(jnp.mean(x * x, axis=-1, keepdims=True) + RMS_EPS)
    o_ref[...] = (x * inv * g_ref[...]).astype(o_ref.dtype)


def _rmsnorm(x, gains, layer, out_dtype, tm):
    m, d = x.shape
    return pl.pallas_call(
        _rmsnorm_body,
        grid=(m // tm,),
        in_specs=[pl.BlockSpec((tm, d), lambda i: (i, 0)),
                  pl.BlockSpec((None, 1, d), lambda i: (layer, 0, 0))],
        out_specs=pl.BlockSpec((tm, d), lambda i: (i, 0)),
        out_shape=jax.ShapeDtypeStruct((m, d), out_dtype),
        compiler_params=_params("arbitrary"),
        name="rmsnorm",
    )(x, gains)


def _stage_weights(w_hbm, stage_ref, wb_ref, sem, layer, tn):
    j, i = pl.program_id(0), pl.program_id(1)
    n_slots = stage_ref.shape[0]

    def fetch(jj):
        cols = pl.ds(pl.multiple_of(jj * tn, tn), tn)
        slot = jj % n_slots
        return pltpu.make_async_copy(w_hbm.at[layer, :, cols], stage_ref.at[slot], sem.at[slot])

    def prefetch_next():
        @pl.when(j + 1 < pl.num_programs(0))
        def _():
            fetch(j + 1).start()

    @pl.when(i == 0)
    def _():
        @pl.when(j == 0)
        def _():
            fetch(0).start()

        if n_slots > 1:
            prefetch_next()
        fetch(j).wait()
        wb_ref[...] = stage_ref[j % n_slots].astype(BF16)
        if n_slots == 1:
            prefetch_next()


def _weight_scratch(kdim, tn, n_row_tiles):
    n_slots = 2 if n_row_tiles == 1 else 1
    return [pltpu.VMEM((n_slots, kdim, tn), F32), pltpu.VMEM((kdim, tn), BF16), pltpu.SemaphoreType.DMA((n_slots,))]


def _proj_body(x_ref, w_hbm, o_ref, stage_ref, wb_ref, sem, *, layer, tn):
    _stage_weights(w_hbm, stage_ref, wb_ref, sem, layer, tn)
    o_ref[...] = jnp.dot(x_ref[...], wb_ref[...], preferred_element_type=F32)


def _project(x, w, layer, tm, tn):
    m, kdim = x.shape
    n = w.shape[-1]
    return pl.pallas_call(
        functools.partial(_proj_body, layer=layer, tn=tn),
        grid=(n // tn, m // tm),
        in_specs=[pl.BlockSpec((tm, kdim), lambda j, i: (i, 0)),
                  pl.BlockSpec(memory_space=pl.ANY)],
        out_specs=pl.BlockSpec((tm, tn), lambda j, i: (i, j)),
        out_shape=jax.ShapeDtypeStruct((m, n), F32),
        scratch_shapes=_weight_scratch(kdim, tn, m // tm),
        compiler_params=_params("arbitrary", "arbitrary"),
        name="in_proj",
    )(x, w)


def _outproj_body(a_ref, c_ref, w_hbm, r_ref, o_ref, stage_ref, wb_ref, sem, *, d_main, layer, tn):
    _stage_weights(w_hbm, stage_ref, wb_ref, sem, layer, tn)
    acc = jnp.dot(a_ref[...], wb_ref[:d_main, :], preferred_element_type=F32)
    acc = acc + jnp.dot(c_ref[...], wb_ref[d_main:, :], preferred_element_type=F32)
    o_ref[...] = r_ref[...] + acc


def _out_project(mix_main, mix_cross, w, layer, resid, tm, tn):
    m, d_main = mix_main.shape
    d_cross = mix_cross.shape[1]
    kdim, n = w.shape[1], w.shape[2]
    return pl.pallas_call(
        functools.partial(_outproj_body, d_main=d_main, layer=layer, tn=tn),
        grid=(n // tn, m // tm),
        in_specs=[pl.BlockSpec((tm, d_main), lambda j, i: (i, 0)),
                  pl.BlockSpec((tm, d_cross), lambda j, i: (i, 0)),
                  pl.BlockSpec(memory_space=pl.ANY),
                  pl.BlockSpec((tm, tn), lambda j, i: (i, j))],
        out_specs=pl.BlockSpec((tm, tn), lambda j, i: (i, j)),
        out_shape=jax.ShapeDtypeStruct((m, n), F32),
        scratch_shapes=_weight_scratch(kdim, tn, m // tm),
        compiler_params=_params("arbitrary", "arbitrary"),
        name="out_proj",
    )(mix_main, mix_cross, w, resid)


def _hgrn_body(q_ref, f_ref, i_ref, g_ref, lbl_ref, og_ref, *rest, layer_j, n_valid, has_s0):
    rest = list(rest)
    s0_ref = rest.pop(0) if has_s0 else None
    o_ref, sfin_ref, mask_s = rest
    c = GLA_CHUNK
    n_rows = q_ref.shape[0]

    n_layers = lbl_ref.shape[0]
    logits = [lbl_ref[i:i + 1, :] for i in range(n_layers)]
    mx = functools.reduce(jnp.maximum, logits)
    es = [jnp.exp(l - mx) for l in logits]
    tot = functools.reduce(lambda a, b: a + b, es)
    ws = [e / tot for e in es]
    csum = ws[0]
    for i in range(1, layer_j + 1):
        csum = csum + ws[i]
    lb = csum - ws[0]
    og = og_ref[...]

    row = lax.broadcasted_iota(jnp.int32, (c, HEAD_DIM), 0)
    tt = lax.broadcasted_iota(jnp.int32, (c, c), 0)
    ss = lax.broadcasted_iota(jnp.int32, (c, c), 1)
    n_levels = (min(n_valid, c) - 1).bit_length()
    for lvl in range(n_levels):
        m = 1 << lvl
        pair = ((tt >> (lvl + 1)) == (ss >> (lvl + 1))) & ((tt & m) != 0) & ((ss & m) == 0)
        mask_s[lvl] = jnp.where(pair, 1.0, 0.0)
    mask_s[n_levels] = jnp.where(tt == ss, 1.0, 0.0)

    def prefix_sum(x):
        s = 1
        while s < c:
            x = x + jnp.where(row >= s, pltpu.roll(x, s, 0), 0.0)
            s *= 2
        return x

    def gates_part(fxc):
        f = lb + (1.0 - lb) * jax.nn.sigmoid(fxc)
        logf = jnp.log(jnp.maximum(f, MIN_FORGET))
        kk = (1.0 - lb) * jax.nn.sigmoid(-fxc)
        if n_valid < c:
            ok = row < n_valid
            logf = jnp.where(ok, logf, 0.0)
            kk = jnp.where(ok, kk, 0.0)
        return kk, prefix_sum(logf)

    def local_part(qc, kk, cum, vc):
        att = _nt_dot(qc, kk) * mask_s[n_levels]
        end = cum
        for lvl in range(n_levels):
            m = 1 << lvl
            if 2 * m >= SUBLANES:
                g = c // (2 * m)
                ref = jnp.broadcast_to(cum.reshape(g, 2 * m, HEAD_DIM)[:, m - 1:m, :],
                                       (g, 2 * m, HEAD_DIM)).reshape(c, HEAD_DIM)
            else:
                upper = (row & m) != 0
                ref = jnp.where(upper, pltpu.roll(end, m, 0), end)
                end = jnp.where(upper, end, pltpu.roll(end, c - m, 0))
            e = jnp.exp2(jnp.abs(cum - ref) * (-LOG2_E))
            att = att + _nt_dot(qc * e, kk * e) * mask_s[lvl]
        last = cum[c - 1:c, :]
        intra = _nn_dot(att, vc)
        q_dec = (qc * jnp.exp(cum)).astype(BF16)
        kv = _nn_dot(vc.T, kk * jnp.exp(last - cum))
        return intra, q_dec, kv, jnp.exp(last)

    def state_part(intra, q_dec, kv, dec, st, gate):
        o = _nt_dot(q_dec, st) + intra
        y = o * lax.rsqrt(jnp.mean(o * o, axis=-1, keepdims=True) + RMS_EPS) * og
        return y * _silu(gate), st * dec + kv

    st0 = s0_ref[...].T if has_s0 else jnp.zeros((HEAD_DIM, HEAD_DIM), F32)
    if n_rows < c:
        pad = jnp.zeros((c - n_rows, HEAD_DIM), F32)
        grow = lambda ref: jnp.concatenate([ref[...], pad], axis=0)
        y, st = state_part(*local_part(grow(q_ref), *gates_part(grow(f_ref)), grow(i_ref)), st0, grow(g_ref))
        o_ref[...] = y[:n_rows].astype(o_ref.dtype)
    else:
        def body(ci, st):
            rows = pl.ds(pl.multiple_of(ci * c, c), c)
            local = local_part(q_ref[rows, :], *gates_part(f_ref[rows, :]), i_ref[rows, :])
            y, st = state_part(*local, st, g_ref[rows, :])
            o_ref[rows, :] = y.astype(o_ref.dtype)
            return st

        st = lax.fori_loop(0, n_rows // c, body, st0, unroll=2)
    sfin_ref[...] = st.T


def _hgrn(proj, lb_logits, out_g, layer_j, n_seq, n_heads, d_main, gate_off, s0, n_valid):
    t = proj.shape[0] // n_seq
    hb = d_main // HEAD_DIM
    col = lambda off: (lambda b, h: (b, off + h))
    in_specs = [pl.BlockSpec((t, HEAD_DIM), col(0)),
                pl.BlockSpec((t, HEAD_DIM), col(hb)),
                pl.BlockSpec((t, HEAD_DIM), col(2 * hb)),
                pl.BlockSpec((t, HEAD_DIM), col(gate_off // HEAD_DIM)),
                pl.BlockSpec((lb_logits.shape[0], HEAD_DIM), lambda b, h: (0, h)),
                pl.BlockSpec((None, 1, HEAD_DIM), lambda b, h: (layer_j, 0, 0))]
    args = [proj, proj, proj, proj, lb_logits, out_g]
    if s0 is not None:
        in_specs.append(pl.BlockSpec((None, None, None, HEAD_DIM, HEAD_DIM),
                                     lambda b, h: (layer_j, b, h, 0, 0)))
        args.append(s0)
    return pl.pallas_call(
        functools.partial(_hgrn_body, layer_j=layer_j, n_valid=n_valid, has_s0=s0 is not None),
        grid=(n_seq, n_heads),
        in_specs=in_specs,
        out_specs=[pl.BlockSpec((t, HEAD_DIM), lambda b, h: (b, h)),
                   pl.BlockSpec((None, None, HEAD_DIM, HEAD_DIM), lambda b, h: (b, h, 0, 0))],
        out_shape=[jax.ShapeDtypeStruct((n_seq * t, d_main), BF16),
                   jax.ShapeDtypeStruct((n_seq, n_heads, HEAD_DIM, HEAD_DIM), F32)],
        scratch_shapes=[pltpu.VMEM((GLA_CHUNK.bit_length(), GLA_CHUNK, GLA_CHUNK), F32)],
        compiler_params=_params("arbitrary", "arbitrary"),
        name="hgrn2",
    )(*args)


def _rope_tables(pos):
    half = HEAD_DIM // 2
    inv = ROPE_THETA ** (-jnp.arange(half, dtype=F32) / half)
    ang = pos.astype(F32)[:, None] * inv[None, :]
    cos, sin = jnp.cos(ang), jnp.sin(ang)
    return jnp.concatenate([cos, cos], axis=-1), jnp.concatenate([-sin, sin], axis=-1)


def _rope(x, cos, sin):
    return x * cos + pltpu.roll(x, HEAD_DIM // 2, 1) * sin


def _moba_body(q_ref, k_ref, v_ref, g_ref, cos_ref, sin_ref, kout_ref, o_ref,
               qf_s, qb_s, kb_s, vt_s, km_s, s_s, p_s):
    blk = MOBA_BLOCK
    nblk = q_ref.shape[0] // blk
    scale = HEAD_DIM ** -0.5

    km_s[...] = jnp.zeros_like(km_s)
    for j in range(nblk):
        rows = pl.ds(j * blk, blk)
        cos, sin = cos_ref[rows, :], sin_ref[rows, :]
        q = _rope(q_ref[rows, :], cos, sin)
        k = _rope(k_ref[rows, :], cos, sin)
        kout_ref[rows, :] = k
        qf_s[j] = q
        qb_s[j] = q.astype(BF16)
        kb_s[rows, :] = k.astype(BF16)
        vt_s[:HEAD_DIM, rows] = v_ref[rows, :].T.astype(BF16)
        vt_s[HEAD_DIM:, rows] = jnp.ones((vt_s.shape[0] - HEAD_DIM, blk), BF16)
        km_s[j:j + 1, :] = jnp.sum(k, axis=0, keepdims=True) * (1.0 / blk)

    npad = km_s.shape[0]
    jidx = lax.broadcasted_iota(jnp.int32, (npad, blk), 0)
    causal = (lax.broadcasted_iota(jnp.int32, (blk, blk), 0) <=
              lax.broadcasted_iota(jnp.int32, (blk, blk), 1))

    first = lambda i: (i * (i + 1) // 2) * blk
    col_max = []
    for i in range(nblk):
        gate = lax.dot_general(km_s[...], qf_s[i], (((1,), (1,)), ((), ())),
                               precision=lax.Precision.HIGHEST, preferred_element_type=F32)
        cand = jidx < i
        gm = jnp.where(cand, gate, -jnp.inf)
        rank = jnp.zeros((npad, blk), jnp.int32)
        for jp in range(i):
            gj = gm[jp:jp + 1, :]
            rank = rank + jnp.where((gj > gm) | ((gj == gm) & (jp < jidx)), 1, 0)
        sel = jnp.where(cand & (rank < MOBA_TOPK) & (jnp.abs(gate) < jnp.inf), 1.0, 0.0)

        m = jnp.full((1, blk), -jnp.inf, F32)
        qi = qb_s[i]
        for j in range(i + 1):
            mask = causal if j == i else jnp.broadcast_to(sel[j:j + 1, :], (blk, blk)) > 0.0
            st = jnp.where(mask, _nt_dot(kb_s[pl.ds(j * blk, blk), :], qi), -jnp.inf)
            s_s[pl.ds(first(i) + j * blk, blk), :] = st
            m = jnp.maximum(m, jnp.max(st, axis=0, keepdims=True))
        col_max.append(m)
    for i in range(nblk):
        for j in range(i + 1):
            rows = pl.ds(first(i) + j * blk, blk)
            p_s[rows, :] = jnp.exp2((s_s[rows, :] - col_max[i]) * (scale * LOG2_E)).astype(BF16)
    for i in range(nblk):
        n = (i + 1) * blk
        acc = jnp.dot(vt_s[:, :n], p_s[pl.ds(first(i), n), :], preferred_element_type=F32)
        out = (acc[:HEAD_DIM] / acc[HEAD_DIM:HEAD_DIM + 1]).T
        rows = pl.ds(i * blk, blk)
        o_ref[rows, :] = (out * _silu(g_ref[rows, :])).astype(o_ref.dtype)


def _moba_prompt(proj, cos, sin, n_seq, n_heads, d_main, gate_off):
    t = proj.shape[0] // n_seq
    assert t % MOBA_BLOCK == 0
    nblk = t // MOBA_BLOCK
    hb = d_main // HEAD_DIM
    col = lambda off: (lambda b, h: (b, off + h))
    tab = pl.BlockSpec((t, HEAD_DIM), lambda b, h: (0, 0))
    return pl.pallas_call(
        _moba_body,
        grid=(n_seq, n_heads),
        in_specs=[pl.BlockSpec((t, HEAD_DIM), col(0)),
                  pl.BlockSpec((t, HEAD_DIM), col(hb)),
                  pl.BlockSpec((t, HEAD_DIM), col(2 * hb)),
                  pl.BlockSpec((t, HEAD_DIM), col(gate_off // HEAD_DIM)),
                  tab, tab],
        out_specs=[pl.BlockSpec((t, HEAD_DIM), lambda b, h: (b, h)),
                   pl.BlockSpec((t, HEAD_DIM), lambda b, h: (b, h))],
        out_shape=[jax.ShapeDtypeStruct((n_seq * t, d_main), F32),
                   jax.ShapeDtypeStruct((n_seq * t, d_main), BF16)],
        scratch_shapes=[pltpu.VMEM((nblk, MOBA_BLOCK, HEAD_DIM), F32),
                        pltpu.VMEM((nblk, MOBA_BLOCK, HEAD_DIM), BF16),
                        pltpu.VMEM((t, HEAD_DIM), BF16),
                        pltpu.VMEM((HEAD_DIM + 2 * SUBLANES, t), BF16),
                        pltpu.VMEM((-(-nblk // SUBLANES) * SUBLANES, HEAD_DIM), F32),
                        pltpu.VMEM((nblk * (nblk + 1) // 2 * MOBA_BLOCK, MOBA_BLOCK), F32),
                        pltpu.VMEM((nblk * (nblk + 1) // 2 * MOBA_BLOCK, MOBA_BLOCK), BF16)],
        compiler_params=_params("arbitrary", "arbitrary"),
        name="moba_prompt",
    )(proj, proj, proj, proj, cos, sin)


def _moba_parts_body(pt_ref, k0_ref, k1_ref, v0_ref, v1_ref, q_ref, kn_ref, vn_ref, cos_ref, sin_ref,
                     po_ref, pm_ref, pl_ref, pg_ref, knew_ref, s_s, qk_s, *, n_heads, n_new, n_past_blocks):
    del pt_ref
    j = pl.program_id(1)
    scale = HEAD_DIM ** -0.5
    cos, sin = cos_ref[...], sin_ref[...]
    rows = q_ref.shape[0]

    def put(ref, h, val):
        ref[:, h * HEAD_DIM:(h + 1) * HEAD_DIM] = jnp.broadcast_to(val, (rows, HEAD_DIM))

    @pl.when(j < n_past_blocks)
    def _():
        for h in range(n_heads):
            hs = slice(h * HEAD_DIM, (h + 1) * HEAD_DIM)
            hr = slice(h * rows, (h + 1) * rows)
            head_rows = pl.ds(h, PAGE_SIZE, stride=n_heads)
            kh = jnp.concatenate([k0_ref[head_rows, :], k1_ref[head_rows, :]], axis=0)
            qh = _rope(q_ref[:, hs], cos, sin)
            s_s[hr, :] = _nt_dot(qh, kh) * scale
            qk_s[hr, :] = qh * (jnp.sum(kh, axis=0, keepdims=True) * (1.0 / MOBA_BLOCK))
        s = s_s[...]
        m = jnp.max(s, axis=1, keepdims=True)
        p = jnp.exp(s - m)
        s_s[...] = p
        lsum = jnp.sum(p, axis=1, keepdims=True)
        gate = jnp.sum(qk_s[...], axis=1, keepdims=True)
        for h in range(n_heads):
            hs = slice(h * HEAD_DIM, (h + 1) * HEAD_DIM)
            hr = slice(h * rows, (h + 1) * rows)
            head_rows = pl.ds(h, PAGE_SIZE, stride=n_heads)
            vh = jnp.concatenate([v0_ref[head_rows, :], v1_ref[head_rows, :]], axis=0)
            po_ref[:, hs] = _nn_dot(s_s[hr, :], vh)
            put(pm_ref, h, m[hr])
            put(pl_ref, h, lsum[hr])
            put(pg_ref, h, gate[hr])

    @pl.when(j == n_past_blocks)
    def _():
        t_i = lax.broadcasted_iota(jnp.int32, (rows, 1), 0)
        for h in range(n_heads):
            hs = slice(h * HEAD_DIM, (h + 1) * HEAD_DIM)
            qh = _rope(q_ref[:, hs], cos, sin)
            kh = _rope(kn_ref[:, hs], cos, sin)
            vh = vn_ref[:, hs]
            knew_ref[:, hs] = kh
            logit = [jnp.where(t_i >= s, jnp.sum(qh * kh[s:s + 1, :], axis=1, keepdims=True) * scale, -jnp.inf)
                     for s in range(n_new)]
            m = functools.reduce(jnp.maximum, logit)
            ps = [jnp.exp(lg - m) for lg in logit]
            po_ref[:, hs] = functools.reduce(lambda a, b: a + b, [p * vh[s:s + 1, :] for s, p in enumerate(ps)])
            put(pm_ref, h, m)
            put(pl_ref, h, functools.reduce(lambda a, b: a + b, ps))
            put(pg_ref, h, jnp.zeros((rows, 1), F32))


def _moba_combine_body(po_ref, pm_ref, pl_ref, pg_ref, g_ref, o_ref, cur_s, sel_s, *, n_past_blocks):
    nb = n_past_blocks
    shape = cur_s.shape
    jidx = lax.broadcasted_iota(jnp.int32, shape, 0)
    cur_s[...] = pg_ref[:nb]
    sel_s[...] = jnp.zeros(shape, F32)
    for _ in range(MOBA_TOPK):
        cur = cur_s[...]
        best = jnp.max(cur, axis=0, keepdims=True)
        first = jnp.min(jnp.where(cur == best, jidx, nb), axis=0, keepdims=True)
        pick = jidx == first
        sel_s[...] = jnp.where(pick & (jnp.abs(best) < jnp.inf), 1.0, sel_s[...])
        cur_s[...] = jnp.where(pick, -jnp.inf, cur)
    sel = sel_s[...] > 0.0
    m_own = pm_ref[nb]
    m_all = jnp.maximum(m_own, jnp.max(jnp.where(sel, pm_ref[:nb], -jnp.inf), axis=0))
    w = jnp.where(sel, jnp.exp(pm_ref[:nb] - m_all[None]), 0.0)
    w_own = jnp.exp(m_own - m_all)
    den = jnp.sum(w * pl_ref[:nb], axis=0) + w_own * pl_ref[nb]
    num = jnp.sum(w * po_ref[:nb], axis=0) + w_own * po_ref[nb]
    o_ref[...] = ((num / den) * _silu(g_ref[...])).astype(o_ref.dtype)


def _moba_decode(proj, k_cache, v_cache, page_table, layer_j, cos, sin, n_seq, n_heads, d_main, gate_off, n_new):
    rows = SAMPLE_ROWS
    n_pages = page_table.shape[1]
    pages_per_block = MOBA_BLOCK // PAGE_SIZE
    assert pages_per_block == 2 and n_pages % pages_per_block == 0
    nb = n_pages // pages_per_block
    page_rows = PAGE_SIZE * n_heads

    def page(p):
        def index(b, j, pt):
            return (layer_j, pt[b, jnp.minimum(pages_per_block * j + p, n_pages - 1)], 0, 0)
        return pl.BlockSpec((None, None, page_rows, HEAD_DIM), index)

    tok = lambda off: pl.BlockSpec((rows, d_main), lambda b, j, pt: (b, off))
    tab = pl.BlockSpec((rows, HEAD_DIM), lambda b, j, pt: (0, 0))
    part = pl.BlockSpec((None, None, rows, d_main), lambda b, j, pt: (b, j, 0, 0))
    part_shape = jax.ShapeDtypeStruct((n_seq, nb + 1, rows, d_main), F32)
    po, pm, pls, pg, k_new = pl.pallas_call(
        functools.partial(_moba_parts_body, n_heads=n_heads, n_new=n_new, n_past_blocks=nb),
        grid_spec=pltpu.PrefetchScalarGridSpec(
            num_scalar_prefetch=1,
            grid=(n_seq, nb + 1),
            in_specs=[page(0), page(1), page(0), page(1), tok(0), tok(1), tok(2), tab, tab],
            out_specs=[part, part, part, part,
                       pl.BlockSpec((rows, d_main), lambda b, j, pt: (b, 0))],
            scratch_shapes=[pltpu.VMEM((n_heads * rows, MOBA_BLOCK), F32),
                            pltpu.VMEM((n_heads * rows, HEAD_DIM), F32)]),
        out_shape=[part_shape, part_shape, part_shape, part_shape,
                   jax.ShapeDtypeStruct((n_seq * rows, d_main), F32)],
        compiler_params=_params("arbitrary", "arbitrary"),
        name="moba_decode_parts",
    )(page_table, k_cache, k_cache, v_cache, v_cache, proj, proj, proj, cos, sin)

    whole = pl.BlockSpec((None, nb + 1, rows, d_main), lambda b: (b, 0, 0, 0))
    gate_blk = 1024
    assert gate_off % gate_blk == 0 and d_main % gate_blk == 0
    n_gate = d_main // gate_blk
    gate_specs = [pl.BlockSpec((rows, gate_blk), functools.partial(lambda b, i: (b, gate_off // gate_blk + i), i=i))
                  for i in range(n_gate)]

    def body(po_ref, pm_ref, pl_ref, pg_ref, *rest):
        g_refs, (o_ref, cur_s, sel_s, g_s) = rest[:n_gate], rest[n_gate:]
        for i, g_ref in enumerate(g_refs):
            g_s[:, i * gate_blk:(i + 1) * gate_blk] = g_ref[...]
        _moba_combine_body(po_ref, pm_ref, pl_ref, pg_ref, g_s, o_ref, cur_s, sel_s, n_past_blocks=nb)

    mix = pl.pallas_call(
        body,
        grid=(n_seq,),
        in_specs=[whole, whole, whole, whole] + gate_specs,
        out_specs=pl.BlockSpec((rows, d_main), lambda b: (b, 0)),
        out_shape=jax.ShapeDtypeStruct((n_seq * rows, d_main), BF16),
        scratch_shapes=[pltpu.VMEM((nb, rows, d_main), F32), pltpu.VMEM((nb, rows, d_main), F32),
                        pltpu.VMEM((rows, d_main), F32)],
        compiler_params=_params("arbitrary"),
        name="moba_decode_combine",
    )(po, pm, pls, pg, *([proj] * n_gate))
    return mix, k_new


def _xattn_body(q_ref, k_ref, v_ref, g_ref, o_ref):
    scale = q_ref.shape[-1] ** -0.5
    s = _nt_dot(q_ref[...], k_ref[...]) * scale
    m = jnp.max(s, axis=-1, keepdims=True)
    p = jnp.exp(s - m)
    p = p / jnp.sum(p, axis=-1, keepdims=True)
    o_ref[...] = (_nn_dot(p, v_ref[...]) * _silu(g_ref[...])).astype(o_ref.dtype)


def _cross_attention(proj, mem_k, mem_v, k_spec, v_spec, n_seq, tq, q_off, gate_off, d_cross):
    t = proj.shape[0] // n_seq
    hd = d_cross // N_MEM_HEADS
    nq = t // tq
    row = lambda off: (lambda b, h, i: (b * nq + i, off + h))
    return pl.pallas_call(
        _xattn_body,
        grid=(n_seq, N_MEM_HEADS, nq),
        in_specs=[pl.BlockSpec((tq, hd), row(q_off // hd)),
                  k_spec, v_spec,
                  pl.BlockSpec((tq, hd), row(gate_off // hd))],
        out_specs=pl.BlockSpec((tq, hd), row(0)),
        out_shape=jax.ShapeDtypeStruct((n_seq * t, d_cross), BF16),
        compiler_params=_params("arbitrary", "arbitrary", "arbitrary"),
        name="mem_xattn",
    )(proj, mem_k, mem_v, proj)


def kernel(x_prompt, x_sample, cache_moba_k, cache_moba_v, state_hgrn, cache_mem_k, cache_mem_v,
           page_table, mem_prompt, norm_g, w_in, w_out, hgrn_lb_logits, hgrn_out_g, mem_norm_g,
           w_mem_kv, final_norm_g):
    bp, tp, d_model = x_prompt.shape
    bs, ts, _ = x_sample.shape
    depth = w_in.shape[0]
    d_mix = w_out.shape[1]
    d_cross = w_mem_kv.shape[2] // 2
    d_main = d_mix - d_cross
    n_heads = d_main // HEAD_DIM
    mem_len = mem_prompt.shape[1]
    mem_hd = d_cross // N_MEM_HEADS
    n_pages = page_table.shape[1]
    past_len = n_pages * PAGE_SIZE
    q_off = 3 * d_main
    gate_off = 3 * d_main + d_cross
    rows = SAMPLE_ROWS
    assert ts <= rows

    gains = norm_g.reshape(depth, 1, d_model)
    mem_gains = mem_norm_g.reshape(depth, 1, d_model)
    out_gains = hgrn_out_g.reshape(-1, 1, HEAD_DIM)

    xp = x_prompt.reshape(bp * tp, d_model)
    xs = jnp.pad(x_sample, ((0, 0), (0, rows - ts), (0, 0))).reshape(bs * rows, d_model)
    mem = mem_prompt.reshape(bp * mem_len, d_model)

    cos_p, sin_p = _rope_tables(jnp.arange(tp, dtype=jnp.int32))
    cos_s, sin_s = _rope_tables(past_len + jnp.arange(rows, dtype=jnp.int32))
    kc = cache_moba_k.reshape(cache_moba_k.shape[0], cache_moba_k.shape[1], PAGE_SIZE * n_heads, HEAD_DIM)
    vc = cache_moba_v.reshape(kc.shape)
    mem_kc = cache_mem_k.reshape(depth, bs, mem_len, d_cross)
    mem_vc = cache_mem_v.reshape(depth, bs, mem_len, d_cross)

    tm = PROJ_ROWS
    outs = {k: [] for k in ("moba_k_p", "moba_v_p", "hgrn_p", "memk_p", "memv_p", "moba_k_s", "moba_v_s", "hgrn_s")}
    for l in range(depth):
        j = l // N_MIXERS
        proj_p = _project(_rmsnorm(xp, gains, l, BF16, NORM_ROWS), w_in, l, tm, PROJ_COLS)
        proj_s = _project(_rmsnorm(xs, gains, l, BF16, bs * rows), w_in, l, bs * rows, PROJ_COLS)
        if l % N_MIXERS == 0:
            mix_p, st_p = _hgrn(proj_p, hgrn_lb_logits, out_gains, j, bp, n_heads, d_main, gate_off, None, GLA_CHUNK)
            mix_s, st_s = _hgrn(proj_s, hgrn_lb_logits, out_gains, j, bs, n_heads, d_main, gate_off, state_hgrn, ts)
            outs["hgrn_p"].append(st_p)
            outs["hgrn_s"].append(st_s)
        else:
            k_p, mix_p = _moba_prompt(proj_p, cos_p, sin_p, bp, n_heads, d_main, gate_off)
            mix_s, k_s = _moba_decode(proj_s, kc, vc, page_table, j, cos_s, sin_s, bs, n_heads, d_main, gate_off, ts)
            outs["moba_k_p"].append(k_p.reshape(bp, tp, n_heads, HEAD_DIM))
            outs["moba_v_p"].append(proj_p[:, 2 * d_main:3 * d_main].reshape(bp, tp, n_heads, HEAD_DIM))
            outs["moba_k_s"].append(k_s.reshape(bs, rows, n_heads, HEAD_DIM)[:, :ts])
            outs["moba_v_s"].append(proj_s[:, 2 * d_main:3 * d_main].reshape(bs, rows, n_heads, HEAD_DIM)[:, :ts])
        kv = _project(_rmsnorm(mem, mem_gains, l, BF16, NORM_ROWS), w_mem_kv, l, tm, PROJ_COLS)
        outs["memk_p"].append(kv[:, :d_cross].reshape(bp, mem_len, N_MEM_HEADS, mem_hd))
        outs["memv_p"].append(kv[:, d_cross:].reshape(bp, mem_len, N_MEM_HEADS, mem_hd))
        cross_p = _cross_attention(
            proj_p, kv, kv,
            pl.BlockSpec((mem_len, mem_hd), lambda b, h, i: (b, h)),
            pl.BlockSpec((mem_len, mem_hd), lambda b, h, i: (b, N_MEM_HEADS + h)),
            bp, 512, q_off, gate_off + d_main, d_cross)
        cached = pl.BlockSpec((None, None, mem_len, mem_hd), functools.partial(lambda b, h, i, l: (l, b, 0, h), l=l))
        cross_s = _cross_attention(proj_s, mem_kc, mem_vc, cached, cached, bs, rows, q_off, gate_off + d_main, d_cross)
        xp = _out_project(mix_p, cross_p, w_out, l, xp, tm, OUT_PROJ_COLS)
        xs = _out_project(mix_s, cross_s, w_out, l, xs, bs * rows, OUT_PROJ_COLS)

    final_g = final_norm_g.reshape(1, 1, d_model)
    y_prompt = _rmsnorm(xp, final_g, 0, F32, NORM_ROWS).reshape(bp, tp, d_model)
    y_sample = _rmsnorm(xs, final_g, 0, F32, bs * rows).reshape(bs, rows, d_model)[:, :ts]
    return (y_prompt, y_sample,
            jnp.stack(outs["moba_k_p"]), jnp.stack(outs["moba_v_p"]), jnp.stack(outs["hgrn_p"]),
            jnp.stack(outs["memk_p"]), jnp.stack(outs["memv_p"]),
            jnp.stack(outs["moba_k_s"]), jnp.stack(outs["moba_v_s"]), jnp.stack(outs["hgrn_s"]))
```

```python
import functools

import jax
import jax.numpy as jnp
from jax import lax
from jax.experimental import pallas as pl
from jax.experimental.pallas import tpu as pltpu

F32 = jnp.float32
BF16 = jnp.bfloat16

HEAD_DIM = 128
N_MEM_HEADS = 4
PAGE_SIZE = 128
MOBA_BLOCK = 256
MOBA_TOPK = 3
N_MIXERS = 2
ROPE_THETA = 10000.0
RMS_EPS = 1e-6
MIN_FORGET = 1e-30
LOG2_E = 1.4426950408889634

LANES = 128
SUBLANES = 8
VMEM_LIMIT_BYTES = 52 * 1024 * 1024

PROJ_ROWS = 512
PROJ_COLS = 1024
OUT_PROJ_COLS = 1024
NORM_ROWS = 256
XATTN_ROWS = 1024
GLA_CHUNK = 128
SAMPLE_ROWS = SUBLANES


def _params(*semantics):
    return pltpu.CompilerParams(dimension_semantics=semantics, vmem_limit_bytes=VMEM_LIMIT_BYTES)


def _nt_dot(a, b):
    return lax.dot_general(a.astype(BF16), b.astype(BF16), (((1,), (1,)), ((), ())),
                           preferred_element_type=F32)


def _nn_dot(a, b):
    return jnp.dot(a.astype(BF16), b.astype(BF16), preferred_element_type=F32)


def _silu(x):
    return x * jax.nn.sigmoid(x)


def _rmsnorm_body(x_ref, g_ref, o_ref):
    x = x_ref[...]
    inv = lax.rsqrt(jnp.mean(x * x, axis=-1, keepdims=True) + RMS_EPS)
    o_ref[...] = (x * inv * g_ref[...]).astype(o_ref.dtype)


def _rmsnorm(x, gains, layer, out_dtype, tm):
    m, d = x.shape
    return pl.pallas_call(
        _rmsnorm_body,
        grid=(m // tm,),
        in_specs=[pl.BlockSpec((tm, d), lambda i: (i, 0)),
                  pl.BlockSpec((None, 1, d), lambda i: (layer, 0, 0))],
        out_specs=pl.BlockSpec((tm, d), lambda i: (i, 0)),
        out_shape=jax.ShapeDtypeStruct((m, d), out_dtype),
        compiler_params=_params("arbitrary"),
        name="rmsnorm",
    )(x, gains)


def _stage_weights(w_hbm, stage_ref, wb_ref, sem, layer, tn):
    j, i = pl.program_id(0), pl.program_id(1)
    n_slots = stage_ref.shape[0]

    def fetch(jj):
        cols = pl.ds(pl.multiple_of(jj * tn, tn), tn)
        slot = jj % n_slots
        return pltpu.make_async_copy(w_hbm.at[layer, :, cols], stage_ref.at[slot], sem.at[slot])

    def prefetch_next():
        @pl.when(j + 1 < pl.num_programs(0))
        def _():
            fetch(j + 1).start()

    @pl.when(i == 0)
    def _():
        @pl.when(j == 0)
        def _():
            fetch(0).start()

        if n_slots > 1:
            prefetch_next()
        fetch(j).wait()
        wb_ref[...] = stage_ref[j % n_slots].astype(BF16)
        if n_slots == 1:
            prefetch_next()


def _weight_scratch(kdim, tn, n_row_tiles):
    n_slots = 2 if n_row_tiles == 1 else 1
    return [pltpu.VMEM((n_slots, kdim, tn), F32), pltpu.VMEM((kdim, tn), BF16), pltpu.SemaphoreType.DMA((n_slots,))]


def _proj_body(x_ref, w_hbm, o_ref, stage_ref, wb_ref, sem, *, layer, tn):
    _stage_weights(w_hbm, stage_ref, wb_ref, sem, layer, tn)
    o_ref[...] = jnp.dot(x_ref[...], wb_ref[...], preferred_element_type=F32)


def _project(x, w, layer, tm, tn):
    m, kdim = x.shape
    n = w.shape[-1]
    return pl.pallas_call(
        functools.partial(_proj_body, layer=layer, tn=tn),
        grid=(n // tn, m // tm),
        in_specs=[pl.BlockSpec((tm, kdim), lambda j, i: (i, 0)),
                  pl.BlockSpec(memory_space=pl.ANY)],
        out_specs=pl.BlockSpec((tm, tn), lambda j, i: (i, j)),
        out_shape=jax.ShapeDtypeStruct((m, n), F32),
        scratch_shapes=_weight_scratch(kdim, tn, m // tm),
        compiler_params=_params("arbitrary", "arbitrary"),
        name="in_proj",
    )(x, w)


def _outproj_body(a_ref, c_ref, w_hbm, r_ref, o_ref, stage_ref, wb_ref, sem, *, d_main, layer, tn):
    _stage_weights(w_hbm, stage_ref, wb_ref, sem, layer, tn)
    acc = jnp.dot(a_ref[...], wb_ref[:d_main, :], preferred_element_type=F32)
    acc = acc + jnp.dot(c_ref[...], wb_ref[d_main:, :], preferred_element_type=F32)
    o_ref[...] = r_ref[...] + acc


def _out_project(mix_main, mix_cross, w, layer, resid, tm, tn):
    m, d_main = mix_main.shape
    d_cross = mix_cross.shape[1]
    kdim, n = w.shape[1], w.shape[2]
    return pl.pallas_call(
        functools.partial(_outproj_body, d_main=d_main, layer=layer, tn=tn),
        grid=(n // tn, m // tm),
        in_specs=[pl.BlockSpec((tm, d_main), lambda j, i: (i, 0)),
                  pl.BlockSpec((tm, d_cross), lambda j, i: (i, 0)),
                  pl.BlockSpec(memory_space=pl.ANY),
                  pl.BlockSpec((tm, tn), lambda j, i: (i, j))],
        out_specs=pl.BlockSpec((tm, tn), lambda j, i: (i, j)),
        out_shape=jax.ShapeDtypeStruct((m, n), F32),
        scratch_shapes=_weight_scratch(kdim, tn, m // tm),
        compiler_params=_params("arbitrary", "arbitrary"),
        name="out_proj",
    )(mix_main, mix_cross, w, resid)


def _hgrn_body(q_ref, f_ref, i_ref, g_ref, lbl_ref, og_ref, *rest, layer_j, n_valid, has_s0):
    rest = list(rest)
    s0_ref = rest.pop(0) if has_s0 else None
    o_ref, sfin_ref, mask_s = rest
    c = GLA_CHUNK
    n_rows = q_ref.shape[0]

    n_layers = lbl_ref.shape[0]
    logits = [lbl_ref[i:i + 1, :] for i in range(n_layers)]
    mx = functools.reduce(jnp.maximum, logits)
    es = [jnp.exp(l - mx) for l in logits]
    tot = functools.reduce(lambda a, b: a + b, es)
    ws = [e / tot for e in es]
    csum = ws[0]
    for i in range(1, layer_j + 1):
        csum = csum + ws[i]
    lb = csum - ws[0]
    og = og_ref[...]

    row = lax.broadcasted_iota(jnp.int32, (c, HEAD_DIM), 0)
    tt = lax.broadcasted_iota(jnp.int32, (c, c), 0)
    ss = lax.broadcasted_iota(jnp.int32, (c, c), 1)
    n_levels = (min(n_valid, c) - 1).bit_length()
    for lvl in range(n_levels):
        m = 1 << lvl
        pair = ((tt >> (lvl + 1)) == (ss >> (lvl + 1))) & ((tt & m) != 0) & ((ss & m) == 0)
        mask_s[lvl] = jnp.where(pair, 1.0, 0.0)
    mask_s[n_levels] = jnp.where(tt == ss, 1.0, 0.0)

    def prefix_sum(x):
        s = 1
        while s < c:
            x = x + jnp.where(row >= s, pltpu.roll(x, s, 0), 0.0)
            s *= 2
        return x

    def gates_part(fxc):
        f = lb + (1.0 - lb) * jax.nn.sigmoid(fxc)
        logf = jnp.log(jnp.maximum(f, MIN_FORGET))
        kk = 1.0 - f
        if n_valid < c:
            ok = row < n_valid
            logf = jnp.where(ok, logf, 0.0)
            kk = jnp.where(ok, kk, 0.0)
        return kk, prefix_sum(logf)

    def local_part(qc, kk, cum, vc):
        att = _nt_dot(qc, kk) * mask_s[n_levels]
        end = cum
        for lvl in range(n_levels):
            m = 1 << lvl
            if 2 * m >= SUBLANES:
                g = c // (2 * m)
                ref = jnp.broadcast_to(cum.reshape(g, 2 * m, HEAD_DIM)[:, m - 1:m, :],
                                       (g, 2 * m, HEAD_DIM)).reshape(c, HEAD_DIM)
            else:
                upper = (row & m) != 0
                ref = jnp.where(upper, pltpu.roll(end, m, 0), end)
                end = jnp.where(upper, end, pltpu.roll(end, c - m, 0))
            e = jnp.exp2(jnp.abs(cum - ref) * (-LOG2_E))
            att = att + _nt_dot(qc * e, kk * e) * mask_s[lvl]
        last = cum[c - 1:c, :]
        intra = _nn_dot(att, vc)
        q_dec = (qc * jnp.exp(cum)).astype(BF16)
        kv = _nn_dot(vc.T, kk * jnp.exp(last - cum))
        return intra, q_dec, kv, jnp.exp(last)

    def state_part(intra, q_dec, kv, dec, st, gate):
        o = _nt_dot(q_dec, st) + intra
        y = o * lax.rsqrt(jnp.mean(o * o, axis=-1, keepdims=True) + RMS_EPS) * og
        return y * _silu(gate), st * dec + kv

    st0 = s0_ref[...].T if has_s0 else jnp.zeros((HEAD_DIM, HEAD_DIM), F32)
    if n_rows < c:
        pad = jnp.zeros((c - n_rows, HEAD_DIM), F32)
        grow = lambda ref: jnp.concatenate([ref[...], pad], axis=0)
        y, st = state_part(*local_part(grow(q_ref), *gates_part(grow(f_ref)), grow(i_ref)), st0, grow(g_ref))
        o_ref[...] = y[:n_rows].astype(o_ref.dtype)
    else:
        def body(ci, st):
            rows = pl.ds(pl.multiple_of(ci * c, c), c)
            local = local_part(q_ref[rows, :], *gates_part(f_ref[rows, :]), i_ref[rows, :])
            y, st = state_part(*local, st, g_ref[rows, :])
            o_ref[rows, :] = y.astype(o_ref.dtype)
            return st

        st = lax.fori_loop(0, n_rows // c, body, st0, unroll=2)
    sfin_ref[...] = st.T


def _hgrn(proj, lb_logits, out_g, layer_j, n_seq, n_heads, d_main, gate_off, s0, n_valid):
    t = proj.shape[0] // n_seq
    hb = d_main // HEAD_DIM
    col = lambda off: (lambda b, h: (b, off + h))
    in_specs = [pl.BlockSpec((t, HEAD_DIM), col(0)),
                pl.BlockSpec((t, HEAD_DIM), col(hb)),
                pl.BlockSpec((t, HEAD_DIM), col(2 * hb)),
                pl.BlockSpec((t, HEAD_DIM), col(gate_off // HEAD_DIM)),
                pl.BlockSpec((lb_logits.shape[0], HEAD_DIM), lambda b, h: (0, h)),
                pl.BlockSpec((None, 1, HEAD_DIM), lambda b, h: (layer_j, 0, 0))]
    args = [proj, proj, proj, proj, lb_logits, out_g]
    if s0 is not None:
        in_specs.append(pl.BlockSpec((None, None, None, HEAD_DIM, HEAD_DIM),
                                     lambda b, h: (layer_j, b, h, 0, 0)))
        args.append(s0)
    return pl.pallas_call(
        functools.partial(_hgrn_body, layer_j=layer_j, n_valid=n_valid, has_s0=s0 is not None),
        grid=(n_seq, n_heads),
        in_specs=in_specs,
        out_specs=[pl.BlockSpec((t, HEAD_DIM), lambda b, h: (b, h)),
                   pl.BlockSpec((None, None, HEAD_DIM, HEAD_DIM), lambda b, h: (b, h, 0, 0))],
        out_shape=[jax.ShapeDtypeStruct((n_seq * t, d_main), BF16),
                   jax.ShapeDtypeStruct((n_seq, n_heads, HEAD_DIM, HEAD_DIM), F32)],
        scratch_shapes=[pltpu.VMEM((GLA_CHUNK.bit_length(), GLA_CHUNK, GLA_CHUNK), F32)],
        compiler_params=_params("arbitrary", "arbitrary"),
        name="hgrn2",
    )(*args)


def _rope_tables(pos):
    half = HEAD_DIM // 2
    inv = ROPE_THETA ** (-jnp.arange(half, dtype=F32) / half)
    ang = pos.astype(F32)[:, None] * inv[None, :]
    cos, sin = jnp.cos(ang), jnp.sin(ang)
    return jnp.concatenate([cos, cos], axis=-1), jnp.concatenate([-sin, sin], axis=-1)


def _rope(x, cos, sin):
    return x * cos + pltpu.roll(x, HEAD_DIM // 2, 1) * sin


def _moba_body(q_ref, k_ref, v_ref, g_ref, cos_ref, sin_ref, kout_ref, vout_ref, o_ref,
               qf_s, qb_s, kb_s, vt_s, km_s, s_s, p_s):
    blk = MOBA_BLOCK
    nblk = q_ref.shape[0] // blk
    scale = HEAD_DIM ** -0.5

    t = q_ref.shape[0]
    head_slot = pl.program_id(1) % SUBLANES
    kout_rows = kout_ref.reshape(t * SUBLANES, HEAD_DIM)
    vout_rows = vout_ref.reshape(t * SUBLANES, HEAD_DIM)

    km_s[...] = jnp.zeros_like(km_s)
    for j in range(nblk):
        rows = pl.ds(j * blk, blk)
        head_rows = pl.ds(j * blk * SUBLANES + head_slot, blk, stride=SUBLANES)
        cos, sin = cos_ref[rows, :], sin_ref[rows, :]
        q = _rope(q_ref[rows, :], cos, sin)
        k = _rope(k_ref[rows, :], cos, sin)
        v = v_ref[rows, :]
        kout_rows[head_rows, :] = k
        vout_rows[head_rows, :] = v
        qf_s[j] = q
        qb_s[j] = q.astype(BF16)
        kb_s[rows, :] = k.astype(BF16)
        vt_s[:HEAD_DIM, rows] = v.T.astype(BF16)
        vt_s[HEAD_DIM:, rows] = jnp.ones((vt_s.shape[0] - HEAD_DIM, blk), BF16)
        km_s[j:j + 1, :] = jnp.sum(k, axis=0, keepdims=True) * (1.0 / blk)

    npad = km_s.shape[0]
    jidx = lax.broadcasted_iota(jnp.int32, (npad, blk), 0)
    causal = (lax.broadcasted_iota(jnp.int32, (blk, blk), 0) <=
              lax.broadcasted_iota(jnp.int32, (blk, blk), 1))

    first = lambda i: (i * (i + 1) // 2) * blk
    col_max = []
    for i in range(nblk):
        gate = lax.dot_general(km_s[...], qf_s[i], (((1,), (1,)), ((), ())),
                               precision=lax.Precision.HIGHEST, preferred_element_type=F32)
        cand = jidx < i
        gm = jnp.where(cand, gate, -jnp.inf)
        rank = jnp.zeros((npad, blk), jnp.int32)
        for jp in range(i):
            gj = gm[jp:jp + 1, :]
            rank = rank + jnp.where((gj > gm) | ((gj == gm) & (jp < jidx)), 1, 0)
        sel = jnp.where(cand & (rank < MOBA_TOPK) & (jnp.abs(gate) < jnp.inf), 1.0, 0.0)

        m = jnp.full((1, blk), -jnp.inf, F32)
        qi = qb_s[i]
        for j in range(i + 1):
            mask = causal if j == i else jnp.broadcast_to(sel[j:j + 1, :], (blk, blk)) > 0.0
            st = jnp.where(mask, _nt_dot(kb_s[pl.ds(j * blk, blk), :], qi), -jnp.inf)
            s_s[pl.ds(first(i) + j * blk, blk), :] = st
            m = jnp.maximum(m, jnp.max(st, axis=0, keepdims=True))
        col_max.append(m)
    for i in range(nblk):
        for j in range(i + 1):
            rows = pl.ds(first(i) + j * blk, blk)
            p_s[rows, :] = jnp.exp2((s_s[rows, :] - col_max[i]) * (scale * LOG2_E)).astype(BF16)
    for i in range(nblk):
        n = (i + 1) * blk
        acc = jnp.dot(vt_s[:, :n], p_s[pl.ds(first(i), n), :], preferred_element_type=F32)
        out = (acc[:HEAD_DIM] / acc[HEAD_DIM:HEAD_DIM + 1]).T
        rows = pl.ds(i * blk, blk)
        o_ref[rows, :] = (out * _silu(g_ref[rows, :])).astype(o_ref.dtype)


def _moba_prompt(proj, cos, sin, n_seq, n_heads, d_main, gate_off):
    t = proj.shape[0] // n_seq
    assert t % MOBA_BLOCK == 0
    nblk = t // MOBA_BLOCK
    hb = d_main // HEAD_DIM
    col = lambda off: (lambda b, h: (b, off + h))
    tab = pl.BlockSpec((t, HEAD_DIM), lambda b, h: (0, 0))
    assert n_heads % SUBLANES == 0
    cache_rows = pl.BlockSpec((None, t, None, SUBLANES, HEAD_DIM), lambda b, h: (b, 0, h // SUBLANES, 0, 0))
    cache_shape = jax.ShapeDtypeStruct((n_seq, t, n_heads // SUBLANES, SUBLANES, HEAD_DIM), F32)
    return pl.pallas_call(
        _moba_body,
        grid=(n_seq, n_heads),
        in_specs=[pl.BlockSpec((t, HEAD_DIM), col(0)),
                  pl.BlockSpec((t, HEAD_DIM), col(hb)),
                  pl.BlockSpec((t, HEAD_DIM), col(2 * hb)),
                  pl.BlockSpec((t, HEAD_DIM), col(gate_off // HEAD_DIM)),
                  tab, tab],
        out_specs=[cache_rows, cache_rows, pl.BlockSpec((t, HEAD_DIM), lambda b, h: (b, h))],
        out_shape=[cache_shape, cache_shape, jax.ShapeDtypeStruct((n_seq * t, d_main), BF16)],
        scratch_shapes=[pltpu.VMEM((nblk, MOBA_BLOCK, HEAD_DIM), F32),
                        pltpu.VMEM((nblk, MOBA_BLOCK, HEAD_DIM), BF16),
                        pltpu.VMEM((t, HEAD_DIM), BF16),
                        pltpu.VMEM((HEAD_DIM + 2 * SUBLANES, t), BF16),
                        pltpu.VMEM((-(-nblk // SUBLANES) * SUBLANES, HEAD_DIM), F32),
                        pltpu.VMEM((nblk * (nblk + 1) // 2 * MOBA_BLOCK, MOBA_BLOCK), F32),
                        pltpu.VMEM((nblk * (nblk + 1) // 2 * MOBA_BLOCK, MOBA_BLOCK), BF16)],
        compiler_params=_params("arbitrary", "arbitrary"),
        name="moba_prompt",
    )(proj, proj, proj, proj, cos, sin)


def _moba_parts_body(pt_ref, k0_ref, k1_ref, v0_ref, v1_ref, q_ref, kn_ref, vn_ref, cos_ref, sin_ref,
                     po_ref, pm_ref, pl_ref, pg_ref, knew_ref, s_s, qk_s, *, n_heads, n_new, n_past_blocks):
    del pt_ref
    j = pl.program_id(1)
    scale = HEAD_DIM ** -0.5
    cos, sin = cos_ref[...], sin_ref[...]
    rows = q_ref.shape[0]

    def put(ref, h, val):
        ref[:, h * HEAD_DIM:(h + 1) * HEAD_DIM] = jnp.broadcast_to(val, (rows, HEAD_DIM))

    @pl.when(j < n_past_blocks)
    def _():
        for h in range(n_heads):
            hs = slice(h * HEAD_DIM, (h + 1) * HEAD_DIM)
            hr = slice(h * rows, (h + 1) * rows)
            head_rows = pl.ds(h, PAGE_SIZE, stride=n_heads)
            kh = jnp.concatenate([k0_ref[head_rows, :], k1_ref[head_rows, :]], axis=0)
            qh = _rope(q_ref[:, hs], cos, sin)
            s_s[hr, :] = _nt_dot(qh, kh) * scale
            qk_s[hr, :] = qh * (jnp.sum(kh, axis=0, keepdims=True) * (1.0 / MOBA_BLOCK))
        s = s_s[...]
        m = jnp.max(s, axis=1, keepdims=True)
        p = jnp.exp(s - m)
        s_s[...] = p
        lsum = jnp.sum(p, axis=1, keepdims=True)
        gate = jnp.sum(qk_s[...], axis=1, keepdims=True)
        for h in range(n_heads):
            hs = slice(h * HEAD_DIM, (h + 1) * HEAD_DIM)
            hr = slice(h * rows, (h + 1) * rows)
            head_rows = pl.ds(h, PAGE_SIZE, stride=n_heads)
            vh = jnp.concatenate([v0_ref[head_rows, :], v1_ref[head_rows, :]], axis=0)
            po_ref[:, hs] = _nn_dot(s_s[hr, :], vh)
            put(pm_ref, h, m[hr])
            put(pl_ref, h, lsum[hr])
            put(pg_ref, h, gate[hr])

    @pl.when(j == n_past_blocks)
    def _():
        t_i = lax.broadcasted_iota(jnp.int32, (rows, 1), 0)
        for h in range(n_heads):
            hs = slice(h * HEAD_DIM, (h + 1) * HEAD_DIM)
            qh = _rope(q_ref[:, hs], cos, sin)
            kh = _rope(kn_ref[:, hs], cos, sin)
            vh = vn_ref[:, hs]
            knew_ref[:, hs] = kh
            logit = [jnp.where(t_i >= s, jnp.sum(qh * kh[s:s + 1, :], axis=1, keepdims=True) * scale, -jnp.inf)
                     for s in range(n_new)]
            m = functools.reduce(jnp.maximum, logit)
            ps = [jnp.exp(lg - m) for lg in logit]
            po_ref[:, hs] = functools.reduce(lambda a, b: a + b, [p * vh[s:s + 1, :] for s, p in enumerate(ps)])
            put(pm_ref, h, m)
            put(pl_ref, h, functools.reduce(lambda a, b: a + b, ps))
            put(pg_ref, h, jnp.zeros((rows, 1), F32))


def _moba_combine_body(po_ref, pm_ref, pl_ref, pg_ref, g_ref, o_ref, cur_s, sel_s, *, n_past_blocks):
    nb = n_past_blocks
    shape = cur_s.shape
    jidx = lax.broadcasted_iota(jnp.int32, shape, 0)
    cur_s[...] = pg_ref[:nb]
    sel_s[...] = jnp.zeros(shape, F32)
    for _ in range(MOBA_TOPK):
        cur = cur_s[...]
        best = jnp.max(cur, axis=0, keepdims=True)
        first = jnp.min(jnp.where(cur == best, jidx, nb), axis=0, keepdims=True)
        pick = jidx == first
        sel_s[...] = jnp.where(pick & (jnp.abs(best) < jnp.inf), 1.0, sel_s[...])
        cur_s[...] = jnp.where(pick, -jnp.inf, cur)
    sel = sel_s[...] > 0.0
    m_own = pm_ref[nb]
    m_all = jnp.maximum(m_own, jnp.max(jnp.where(sel, pm_ref[:nb], -jnp.inf), axis=0))
    w = jnp.where(sel, jnp.exp(pm_ref[:nb] - m_all[None]), 0.0)
    w_own = jnp.exp(m_own - m_all)
    den = jnp.sum(w * pl_ref[:nb], axis=0) + w_own * pl_ref[nb]
    num = jnp.sum(w * po_ref[:nb], axis=0) + w_own * po_ref[nb]
    o_ref[...] = ((num / den) * _silu(g_ref[...])).astype(o_ref.dtype)


def _moba_decode(proj, k_cache, v_cache, page_table, layer_j, cos, sin, n_seq, n_heads, d_main, gate_off, n_new):
    rows = SAMPLE_ROWS
    n_pages = page_table.shape[1]
    pages_per_block = MOBA_BLOCK // PAGE_SIZE
    assert pages_per_block == 2 and n_pages % pages_per_block == 0
    nb = n_pages // pages_per_block
    page_rows = PAGE_SIZE * n_heads

    def page(p):
        def index(b, j, pt):
            return (layer_j, pt[b, jnp.minimum(pages_per_block * j + p, n_pages - 1)], 0, 0)
        return pl.BlockSpec((None, None, page_rows, HEAD_DIM), index)

    tok = lambda off: pl.BlockSpec((rows, d_main), lambda b, j, pt: (b, off))
    tab = pl.BlockSpec((rows, HEAD_DIM), lambda b, j, pt: (0, 0))
    part = pl.BlockSpec((None, None, rows, d_main), lambda b, j, pt: (b, j, 0, 0))
    part_shape = jax.ShapeDtypeStruct((n_seq, nb + 1, rows, d_main), F32)
    po, pm, pls, pg, k_new = pl.pallas_call(
        functools.partial(_moba_parts_body, n_heads=n_heads, n_new=n_new, n_past_blocks=nb),
        grid_spec=pltpu.PrefetchScalarGridSpec(
            num_scalar_prefetch=1,
            grid=(n_seq, nb + 1),
            in_specs=[page(0), page(1), page(0), page(1), tok(0), tok(1), tok(2), tab, tab],
            out_specs=[part, part, part, part,
                       pl.BlockSpec((rows, d_main), lambda b, j, pt: (b, 0))],
            scratch_shapes=[pltpu.VMEM((n_heads * rows, MOBA_BLOCK), F32),
                            pltpu.VMEM((n_heads * rows, HEAD_DIM), F32)]),
        out_shape=[part_shape, part_shape, part_shape, part_shape,
                   jax.ShapeDtypeStruct((n_seq * rows, d_main), F32)],
        compiler_params=_params("arbitrary", "arbitrary"),
        name="moba_decode_parts",
    )(page_table, k_cache, k_cache, v_cache, v_cache, proj, proj, proj, cos, sin)

    whole = pl.BlockSpec((None, nb + 1, rows, d_main), lambda b: (b, 0, 0, 0))
    gate_blk = 1024
    assert gate_off % gate_blk == 0 and d_main % gate_blk == 0
    n_gate = d_main // gate_blk
    gate_specs = [pl.BlockSpec((rows, gate_blk), functools.partial(lambda b, i: (b, gate_off // gate_blk + i), i=i))
                  for i in range(n_gate)]

    def body(po_ref, pm_ref, pl_ref, pg_ref, *rest):
        g_refs, (o_ref, cur_s, sel_s, g_s) = rest[:n_gate], rest[n_gate:]
        for i, g_ref in enumerate(g_refs):
            g_s[:, i * gate_blk:(i + 1) * gate_blk] = g_ref[...]
        _moba_combine_body(po_ref, pm_ref, pl_ref, pg_ref, g_s, o_ref, cur_s, sel_s, n_past_blocks=nb)

    mix = pl.pallas_call(
        body,
        grid=(n_seq,),
        in_specs=[whole, whole, whole, whole] + gate_specs,
        out_specs=pl.BlockSpec((rows, d_main), lambda b: (b, 0)),
        out_shape=jax.ShapeDtypeStruct((n_seq * rows, d_main), BF16),
        scratch_shapes=[pltpu.VMEM((nb, rows, d_main), F32), pltpu.VMEM((nb, rows, d_main), F32),
                        pltpu.VMEM((rows, d_main), F32)],
        compiler_params=_params("arbitrary"),
        name="moba_decode_combine",
    )(po, pm, pls, pg, *([proj] * n_gate))
    return mix, k_new


def _xattn_body(q_ref, k_ref, v_ref, g_ref, o_ref):
    scale = q_ref.shape[-1] ** -0.5
    s = _nt_dot(q_ref[...], k_ref[...]) * scale
    m = jnp.max(s, axis=-1, keepdims=True)
    p = jnp.exp(s - m)
    p = p / jnp.sum(p, axis=-1, keepdims=True)
    o_ref[...] = (_nn_dot(p, v_ref[...]) * _silu(g_ref[...])).astype(o_ref.dtype)


def _cross_attention(proj, mem_k, mem_v, k_spec, v_spec, n_seq, tq, q_off, gate_off, d_cross):
    t = proj.shape[0] // n_seq
    hd = d_cross // N_MEM_HEADS
    nq = t // tq
    row = lambda off: (lambda b, h, i: (b * nq + i, off + h))
    return pl.pallas_call(
        _xattn_body,
        grid=(n_seq, N_MEM_HEADS, nq),
        in_specs=[pl.BlockSpec((tq, hd), row(q_off // hd)),
                  k_spec, v_spec,
                  pl.BlockSpec((tq, hd), row(gate_off // hd))],
        out_specs=pl.BlockSpec((tq, hd), row(0)),
        out_shape=jax.ShapeDtypeStruct((n_seq * t, d_cross), BF16),
        compiler_params=_params("arbitrary", "arbitrary", "arbitrary"),
        name="mem_xattn",
    )(proj, mem_k, mem_v, proj)


def kernel(x_prompt, x_sample, cache_moba_k, cache_moba_v, state_hgrn, cache_mem_k, cache_mem_v,
           page_table, mem_prompt, norm_g, w_in, w_out, hgrn_lb_logits, hgrn_out_g, mem_norm_g,
           w_mem_kv, final_norm_g):
    bp, tp, d_model = x_prompt.shape
    bs, ts, _ = x_sample.shape
    depth = w_in.shape[0]
    d_mix = w_out.shape[1]
    d_cross = w_mem_kv.shape[2] // 2
    d_main = d_mix - d_cross
    n_heads = d_main // HEAD_DIM
    mem_len = mem_prompt.shape[1]
    mem_hd = d_cross // N_MEM_HEADS
    n_pages = page_table.shape[1]
    past_len = n_pages * PAGE_SIZE
    q_off = 3 * d_main
    gate_off = 3 * d_main + d_cross
    rows = SAMPLE_ROWS
    assert ts <= rows

    gains = norm_g.reshape(depth, 1, d_model)
    mem_gains = mem_norm_g.reshape(depth, 1, d_model)
    out_gains = hgrn_out_g.reshape(-1, 1, HEAD_DIM)

    xp = x_prompt.reshape(bp * tp, d_model)
    xs = jnp.pad(x_sample, ((0, 0), (0, rows - ts), (0, 0))).reshape(bs * rows, d_model)
    mem = mem_prompt.reshape(bp * mem_len, d_model)

    cos_p, sin_p = _rope_tables(jnp.arange(tp, dtype=jnp.int32))
    cos_s, sin_s = _rope_tables(past_len + jnp.arange(rows, dtype=jnp.int32))
    kc = cache_moba_k.reshape(cache_moba_k.shape[0], cache_moba_k.shape[1], PAGE_SIZE * n_heads, HEAD_DIM)
    vc = cache_moba_v.reshape(kc.shape)
    mem_kc = cache_mem_k.reshape(depth, bs, mem_len, d_cross)
    mem_vc = cache_mem_v.reshape(depth, bs, mem_len, d_cross)

    tm = PROJ_ROWS
    outs = {k: [] for k in ("moba_k_p", "moba_v_p", "hgrn_p", "memk_p", "memv_p", "moba_k_s", "moba_v_s", "hgrn_s")}
    for l in range(depth):
        j = l // N_MIXERS
        proj_p = _project(_rmsnorm(xp, gains, l, BF16, NORM_ROWS), w_in, l, tm, PROJ_COLS)
        proj_s = _project(_rmsnorm(xs, gains, l, BF16, bs * rows), w_in, l, bs * rows, PROJ_COLS)
        if l % N_MIXERS == 0:
            mix_p, st_p = _hgrn(proj_p, hgrn_lb_logits, out_gains, j, bp, n_heads, d_main, gate_off, None, GLA_CHUNK)
            mix_s, st_s = _hgrn(proj_s, hgrn_lb_logits, out_gains, j, bs, n_heads, d_main, gate_off, state_hgrn, ts)
            outs["hgrn_p"].append(st_p)
            outs["hgrn_s"].append(st_s)
        else:
            k_p, v_p, mix_p = _moba_prompt(proj_p, cos_p, sin_p, bp, n_heads, d_main, gate_off)
            mix_s, k_s = _moba_decode(proj_s, kc, vc, page_table, j, cos_s, sin_s, bs, n_heads, d_main, gate_off, ts)
            outs["moba_k_p"].append(k_p.reshape(bp, tp, n_heads, HEAD_DIM))
            outs["moba_v_p"].append(v_p.reshape(bp, tp, n_heads, HEAD_DIM))
            outs["moba_k_s"].append(k_s.reshape(bs, rows, n_heads, HEAD_DIM)[:, :ts])
            outs["moba_v_s"].append(proj_s[:, 2 * d_main:3 * d_main].reshape(bs, rows, n_heads, HEAD_DIM)[:, :ts])
        kv = _project(_rmsnorm(mem, mem_gains, l, BF16, NORM_ROWS), w_mem_kv, l, tm, PROJ_COLS)
        outs["memk_p"].append(kv[:, :d_cross].reshape(bp, mem_len, N_MEM_HEADS, mem_hd))
        outs["memv_p"].append(kv[:, d_cross:].reshape(bp, mem_len, N_MEM_HEADS, mem_hd))
        cross_p = _cross_attention(
            proj_p, kv, kv,
            pl.BlockSpec((mem_len, mem_hd), lambda b, h, i: (b, h)),
            pl.BlockSpec((mem_len, mem_hd), lambda b, h, i: (b, N_MEM_HEADS + h)),
            bp, XATTN_ROWS, q_off, gate_off + d_main, d_cross)
        cached = pl.BlockSpec((None, None, mem_len, mem_hd), functools.partial(lambda b, h, i, l: (l, b, 0, h), l=l))
        cross_s = _cross_attention(proj_s, mem_kc, mem_vc, cached, cached, bs, rows, q_off, gate_off + d_main, d_cross)
        xp = _out_project(mix_p, cross_p, w_out, l, xp, tm, OUT_PROJ_COLS)
        xs = _out_project(mix_s, cross_s, w_out, l, xs, bs * rows, OUT_PROJ_COLS)

    final_g = final_norm_g.reshape(1, 1, d_model)
    y_prompt = _rmsnorm(xp, final_g, 0, F32, NORM_ROWS).reshape(bp, tp, d_model)
    y_sample = _rmsnorm(xs, final_g, 0, F32, bs * rows).reshape(bs, rows, d_model)[:, :ts]
    return (y_prompt, y_sample,
            jnp.stack(outs["moba_k_p"]), jnp.stack(outs["moba_v_p"]), jnp.stack(outs["hgrn_p"]),
            jnp.stack(outs["memk_p"]), jnp.stack(outs["memv_p"]),
            jnp.stack(outs["moba_k_s"]), jnp.stack(outs["moba_v_s"]), jnp.stack(outs["hgrn_s"]))
```

```python
import functools

import jax
import jax.numpy as jnp
from jax import lax
from jax.experimental import pallas as pl
from jax.experimental.pallas import tpu as pltpu

F32 = jnp.float32
BF16 = jnp.bfloat16

HEAD_DIM = 128
N_MEM_HEADS = 4
PAGE_SIZE = 128
MOBA_BLOCK = 256
MOBA_TOPK = 3
N_MIXERS = 2
ROPE_THETA = 10000.0
RMS_EPS = 1e-6
MIN_FORGET = 1e-30
LOG2_E = 1.4426950408889634

LANES = 128
SUBLANES = 8
VMEM_LIMIT_BYTES = 52 * 1024 * 1024

PROJ_ROWS = 512
OUT_PROJ_ROWS = 512
MEM_PROJ_ROWS = 512
PROJ_COLS = 1024
OUT_PROJ_COLS = 1024
NORM_ROWS = 256
XATTN_ROWS = 1024
GLA_CHUNK = 128
SAMPLE_ROWS = SUBLANES


def _params(*semantics):
    return pltpu.CompilerParams(dimension_semantics=semantics, vmem_limit_bytes=VMEM_LIMIT_BYTES)


def _nt_dot(a, b):
    return lax.dot_general(a.astype(BF16), b.astype(BF16), (((1,), (1,)), ((), ())),
                           preferred_element_type=F32)


def _nn_dot(a, b):
    return jnp.dot(a.astype(BF16), b.astype(BF16), preferred_element_type=F32)


def _silu(x):
    return x * jax.nn.sigmoid(x)


def _rmsnorm_body(x_ref, g_ref, o_ref):
    x = x_ref[...]
    inv = lax.rsqrt(jnp.mean(x * x, axis=-1, keepdims=True) + RMS_EPS)
    o_ref[...] = (x * inv * g_ref[...]).astype(o_ref.dtype)


def _rmsnorm(x, gains, layer, out_dtype, tm):
    m, d = x.shape
    return pl.pallas_call(
        _rmsnorm_body,
        grid=(m // tm,),
        in_specs=[pl.BlockSpec((tm, d), lambda i: (i, 0)),
                  pl.BlockSpec((None, 1, d), lambda i: (layer, 0, 0))],
        out_specs=pl.BlockSpec((tm, d), lambda i: (i, 0)),
        out_shape=jax.ShapeDtypeStruct((m, d), out_dtype),
        compiler_params=_params("arbitrary"),
        name="rmsnorm",
    )(x, gains)


def _stage_weights(w_hbm, stage_ref, wb_ref, sem, layer, tn, on_new_block=None):
    j, i = pl.program_id(0), pl.program_id(1)
    n_slots = stage_ref.shape[0]

    def fetch(jj):
        cols = pl.ds(pl.multiple_of(jj * tn, tn), tn)
        slot = jj % n_slots
        return pltpu.make_async_copy(w_hbm.at[layer, :, cols], stage_ref.at[slot], sem.at[slot])

    def prefetch_next():
        @pl.when(j + 1 < pl.num_programs(0))
        def _():
            fetch(j + 1).start()

    @pl.when(i == 0)
    def _():
        @pl.when(j == 0)
        def _():
            fetch(0).start()

        if n_slots > 1:
            prefetch_next()
        fetch(j).wait()
        wb_ref[...] = stage_ref[j % n_slots].astype(BF16)
        if n_slots == 1:
            prefetch_next()
        if on_new_block is not None:
            on_new_block()


def _weight_scratch(kdim, tn, n_row_tiles):
    n_slots = 2 if n_row_tiles == 1 else 1
    return [pltpu.VMEM((n_slots, kdim, tn), F32), pltpu.VMEM((kdim, tn), BF16), pltpu.SemaphoreType.DMA((n_slots,))]


def _proj_body(*refs, layer, tn, with_extra):
    if with_extra:
        x_ref, xe_ref, w_hbm, o_ref, oe_ref, stage_ref, wb_ref, sem = refs
    else:
        x_ref, w_hbm, o_ref, stage_ref, wb_ref, sem = refs

    def extra_rows():
        oe_ref[...] = jnp.dot(xe_ref[...], wb_ref[...], preferred_element_type=F32)

    _stage_weights(w_hbm, stage_ref, wb_ref, sem, layer, tn, extra_rows if with_extra else None)
    o_ref[...] = jnp.dot(x_ref[...], wb_ref[...], preferred_element_type=F32)


def _project(x, w, layer, tm, tn, extra=None):
    m, kdim = x.shape
    n = w.shape[-1]
    in_specs = [pl.BlockSpec((tm, kdim), lambda j, i: (i, 0))]
    out_specs = [pl.BlockSpec((tm, tn), lambda j, i: (i, j))]
    out_shape = [jax.ShapeDtypeStruct((m, n), F32)]
    args = [x]
    if extra is not None:
        me = extra.shape[0]
        in_specs.append(pl.BlockSpec((me, kdim), lambda j, i: (0, 0)))
        out_specs.append(pl.BlockSpec((me, tn), lambda j, i: (0, j)))
        out_shape.append(jax.ShapeDtypeStruct((me, n), F32))
        args.append(extra)
    outs = pl.pallas_call(
        functools.partial(_proj_body, layer=layer, tn=tn, with_extra=extra is not None),
        grid=(n // tn, m // tm),
        in_specs=in_specs + [pl.BlockSpec(memory_space=pl.ANY)],
        out_specs=out_specs,
        out_shape=out_shape,
        scratch_shapes=_weight_scratch(kdim, tn, m // tm),
        compiler_params=_params("arbitrary", "arbitrary"),
        name="in_proj",
    )(*args, w)
    return outs if extra is not None else outs[0]


def _outproj_body(*refs, d_main, layer, tn, with_extra):
    if with_extra:
        a_ref, c_ref, r_ref, ae_ref, ce_ref, re_ref, w_hbm, o_ref, oe_ref, stage_ref, wb_ref, sem = refs
    else:
        a_ref, c_ref, r_ref, w_hbm, o_ref, stage_ref, wb_ref, sem = refs

    def rows(a, c, r):
        acc = jnp.dot(a[...], wb_ref[:d_main, :], preferred_element_type=F32)
        acc = acc + jnp.dot(c[...], wb_ref[d_main:, :], preferred_element_type=F32)
        return r[...] + acc

    def extra_rows():
        oe_ref[...] = rows(ae_ref, ce_ref, re_ref)

    _stage_weights(w_hbm, stage_ref, wb_ref, sem, layer, tn, extra_rows if with_extra else None)
    o_ref[...] = rows(a_ref, c_ref, r_ref)


def _out_project(mix_main, mix_cross, w, layer, resid, tm, tn, extra=None):
    m, d_main = mix_main.shape
    d_cross = mix_cross.shape[1]
    kdim, n = w.shape[1], w.shape[2]

    def specs(rows_per_step, row_index):
        return [pl.BlockSpec((rows_per_step, d_main), lambda j, i: (row_index(i), 0)),
                pl.BlockSpec((rows_per_step, d_cross), lambda j, i: (row_index(i), 0)),
                pl.BlockSpec((rows_per_step, tn), lambda j, i: (row_index(i), j))]

    in_specs = specs(tm, lambda i: i)
    out_specs = [pl.BlockSpec((tm, tn), lambda j, i: (i, j))]
    out_shape = [jax.ShapeDtypeStruct((m, n), F32)]
    args = [mix_main, mix_cross, resid]
    if extra is not None:
        me = extra[0].shape[0]
        in_specs += specs(me, lambda i: 0)
        out_specs.append(pl.BlockSpec((me, tn), lambda j, i: (0, j)))
        out_shape.append(jax.ShapeDtypeStruct((me, n), F32))
        args += list(extra)
    outs = pl.pallas_call(
        functools.partial(_outproj_body, d_main=d_main, layer=layer, tn=tn, with_extra=extra is not None),
        grid=(n // tn, m // tm),
        in_specs=in_specs + [pl.BlockSpec(memory_space=pl.ANY)],
        out_specs=out_specs,
        out_shape=out_shape,
        scratch_shapes=_weight_scratch(kdim, tn, m // tm),
        compiler_params=_params("arbitrary", "arbitrary"),
        name="out_proj",
    )(*args, w)
    return outs if extra is not None else outs[0]


def _hgrn_body(q_ref, f_ref, i_ref, g_ref, lbl_ref, og_ref, *rest, layer_j, n_valid, has_s0):
    rest = list(rest)
    s0_ref = rest.pop(0) if has_s0 else None
    o_ref, sfin_ref, mask_s = rest
    c = GLA_CHUNK
    n_rows = q_ref.shape[0]

    n_layers = lbl_ref.shape[0]
    logits = [lbl_ref[i:i + 1, :] for i in range(n_layers)]
    mx = functools.reduce(jnp.maximum, logits)
    es = [jnp.exp(l - mx) for l in logits]
    tot = functools.reduce(lambda a, b: a + b, es)
    ws = [e / tot for e in es]
    csum = ws[0]
    for i in range(1, layer_j + 1):
        csum = csum + ws[i]
    lb = csum - ws[0]
    og = og_ref[...]

    row = lax.broadcasted_iota(jnp.int32, (c, HEAD_DIM), 0)
    tt = lax.broadcasted_iota(jnp.int32, (c, c), 0)
    ss = lax.broadcasted_iota(jnp.int32, (c, c), 1)
    n_levels = (min(n_valid, c) - 1).bit_length()
    for lvl in range(n_levels):
        m = 1 << lvl
        pair = ((tt >> (lvl + 1)) == (ss >> (lvl + 1))) & ((tt & m) != 0) & ((ss & m) == 0)
        mask_s[lvl] = jnp.where(pair, 1.0, 0.0)
    mask_s[n_levels] = jnp.where(tt == ss, 1.0, 0.0)

    def prefix_sum(x):
        s = 1
        while s < c:
            x = x + jnp.where(row >= s, pltpu.roll(x, s, 0), 0.0)
            s *= 2
        return x

    def gates_part(fxc):
        f = lb + (1.0 - lb) * jax.nn.sigmoid(fxc)
        logf = jnp.log(jnp.maximum(f, MIN_FORGET))
        kk = 1.0 - f
        if n_valid < c:
            ok = row < n_valid
            logf = jnp.where(ok, logf, 0.0)
            kk = jnp.where(ok, kk, 0.0)
        return kk, prefix_sum(logf)

    def local_part(qc, kk, cum, vc):
        att = _nt_dot(qc, kk) * mask_s[n_levels]
        end = cum
        for lvl in range(n_levels):
            m = 1 << lvl
            if 2 * m >= SUBLANES:
                g = c // (2 * m)
                ref = jnp.broadcast_to(cum.reshape(g, 2 * m, HEAD_DIM)[:, m - 1:m, :],
                                       (g, 2 * m, HEAD_DIM)).reshape(c, HEAD_DIM)
            else:
                upper = (row & m) != 0
                ref = jnp.where(upper, pltpu.roll(end, m, 0), end)
                end = jnp.where(upper, end, pltpu.roll(end, c - m, 0))
            e = jnp.exp2(jnp.abs(cum - ref) * (-LOG2_E))
            att = att + _nt_dot(qc * e, kk * e) * mask_s[lvl]
        last = cum[c - 1:c, :]
        intra = _nn_dot(att, vc)
        q_dec = (qc * jnp.exp(cum)).astype(BF16)
        kv = _nn_dot(vc.T, kk * jnp.exp(last - cum))
        return intra, q_dec, kv, jnp.exp(last)

    def state_part(intra, q_dec, kv, dec, st, gate):
        o = _nt_dot(q_dec, st) + intra
        y = o * lax.rsqrt(jnp.mean(o * o, axis=-1, keepdims=True) + RMS_EPS) * og
        return y * _silu(gate), st * dec + kv

    st0 = s0_ref[...].T if has_s0 else jnp.zeros((HEAD_DIM, HEAD_DIM), F32)
    if n_rows < c:
        pad = jnp.zeros((c - n_rows, HEAD_DIM), F32)
        grow = lambda ref: jnp.concatenate([ref[...], pad], axis=0)
        y, st = state_part(*local_part(grow(q_ref), *gates_part(grow(f_ref)), grow(i_ref)), st0, grow(g_ref))
        o_ref[...] = y[:n_rows].astype(o_ref.dtype)
    else:
        def body(ci, st):
            rows = pl.ds(pl.multiple_of(ci * c, c), c)
            local = local_part(q_ref[rows, :], *gates_part(f_ref[rows, :]), i_ref[rows, :])
            y, st = state_part(*local, st, g_ref[rows, :])
            o_ref[rows, :] = y.astype(o_ref.dtype)
            return st

        st = lax.fori_loop(0, n_rows // c, body, st0, unroll=2)
    sfin_ref[...] = st.T


def _hgrn(proj, lb_logits, out_g, layer_j, n_seq, n_heads, d_main, gate_off, s0, n_valid):
    t = proj.shape[0] // n_seq
    hb = d_main // HEAD_DIM
    col = lambda off: (lambda b, h: (b, off + h))
    in_specs = [pl.BlockSpec((t, HEAD_DIM), col(0)),
                pl.BlockSpec((t, HEAD_DIM), col(hb)),
                pl.BlockSpec((t, HEAD_DIM), col(2 * hb)),
                pl.BlockSpec((t, HEAD_DIM), col(gate_off // HEAD_DIM)),
                pl.BlockSpec((lb_logits.shape[0], HEAD_DIM), lambda b, h: (0, h)),
                pl.BlockSpec((None, 1, HEAD_DIM), lambda b, h: (layer_j, 0, 0))]
    args = [proj, proj, proj, proj, lb_logits, out_g]
    if s0 is not None:
        in_specs.append(pl.BlockSpec((None, None, None, HEAD_DIM, HEAD_DIM),
                                     lambda b, h: (layer_j, b, h, 0, 0)))
        args.append(s0)
    return pl.pallas_call(
        functools.partial(_hgrn_body, layer_j=layer_j, n_valid=n_valid, has_s0=s0 is not None),
        grid=(n_seq, n_heads),
        in_specs=in_specs,
        out_specs=[pl.BlockSpec((t, HEAD_DIM), lambda b, h: (b, h)),
                   pl.BlockSpec((None, None, HEAD_DIM, HEAD_DIM), lambda b, h: (b, h, 0, 0))],
        out_shape=[jax.ShapeDtypeStruct((n_seq * t, d_main), BF16),
                   jax.ShapeDtypeStruct((n_seq, n_heads, HEAD_DIM, HEAD_DIM), F32)],
        scratch_shapes=[pltpu.VMEM((GLA_CHUNK.bit_length(), GLA_CHUNK, GLA_CHUNK), F32)],
        compiler_params=_params("arbitrary", "arbitrary"),
        name="hgrn2",
    )(*args)


def _rope_tables(pos):
    half = HEAD_DIM // 2
    inv = ROPE_THETA ** (-jnp.arange(half, dtype=F32) / half)
    ang = pos.astype(F32)[:, None] * inv[None, :]
    cos, sin = jnp.cos(ang), jnp.sin(ang)
    return jnp.concatenate([cos, cos], axis=-1), jnp.concatenate([-sin, sin], axis=-1)


def _rope(x, cos, sin):
    return x * cos + pltpu.roll(x, HEAD_DIM // 2, 1) * sin


def _moba_body(q_ref, k_ref, v_ref, g_ref, cos_ref, sin_ref, kout_ref, vout_ref, o_ref,
               qf_s, qb_s, kb_s, vt_s, km_s, s_s, p_s):
    blk = MOBA_BLOCK
    nblk = q_ref.shape[0] // blk
    scale = HEAD_DIM ** -0.5

    t = q_ref.shape[0]
    head_slot = pl.program_id(1) % SUBLANES
    kout_rows = kout_ref.reshape(t * SUBLANES, HEAD_DIM)
    vout_rows = vout_ref.reshape(t * SUBLANES, HEAD_DIM)

    km_s[...] = jnp.zeros_like(km_s)
    for j in range(nblk):
        rows = pl.ds(j * blk, blk)
        head_rows = pl.ds(j * blk * SUBLANES + head_slot, blk, stride=SUBLANES)
        cos, sin = cos_ref[rows, :], sin_ref[rows, :]
        q = _rope(q_ref[rows, :], cos, sin)
        k = _rope(k_ref[rows, :], cos, sin)
        v = v_ref[rows, :]
        kout_rows[head_rows, :] = k
        vout_rows[head_rows, :] = v
        qf_s[j] = q
        qb_s[j] = q.astype(BF16)
        kb_s[rows, :] = k.astype(BF16)
        vt_s[:HEAD_DIM, rows] = v.T.astype(BF16)
        vt_s[HEAD_DIM:, rows] = jnp.ones((vt_s.shape[0] - HEAD_DIM, blk), BF16)
        km_s[j:j + 1, :] = jnp.sum(k, axis=0, keepdims=True) * (1.0 / blk)

    npad = km_s.shape[0]
    jidx = lax.broadcasted_iota(jnp.int32, (npad, blk), 0)
    causal = (lax.broadcasted_iota(jnp.int32, (blk, blk), 0) <=
              lax.broadcasted_iota(jnp.int32, (blk, blk), 1))

    first = lambda i: (i * (i + 1) // 2) * blk
    col_max = []
    for i in range(nblk):
        gate = lax.dot_general(km_s[...], qf_s[i], (((1,), (1,)), ((), ())),
                               precision=lax.Precision.HIGHEST, preferred_element_type=F32)
        cand = jidx < i
        gm = jnp.where(cand, gate, -jnp.inf)
        rank = jnp.zeros((npad, blk), jnp.int32)
        for jp in range(i):
            gj = gm[jp:jp + 1, :]
            rank = rank + jnp.where((gj > gm) | ((gj == gm) & (jp < jidx)), 1, 0)
        sel = jnp.where(cand & (rank < MOBA_TOPK) & (jnp.abs(gate) < jnp.inf), 1.0, 0.0)

        m = jnp.full((1, blk), -jnp.inf, F32)
        qi = qb_s[i]
        for j in range(i + 1):
            mask = causal if j == i else jnp.broadcast_to(sel[j:j + 1, :], (blk, blk)) > 0.0
            st = jnp.where(mask, _nt_dot(kb_s[pl.ds(j * blk, blk), :], qi), -jnp.inf)
            s_s[pl.ds(first(i) + j * blk, blk), :] = st
            m = jnp.maximum(m, jnp.max(st, axis=0, keepdims=True))
        col_max.append(m)
    for i in range(nblk):
        for j in range(i + 1):
            rows = pl.ds(first(i) + j * blk, blk)
            p_s[rows, :] = jnp.exp2((s_s[rows, :] - col_max[i]) * (scale * LOG2_E)).astype(BF16)
    for i in range(nblk):
        n = (i + 1) * blk
        acc = jnp.dot(vt_s[:, :n], p_s[pl.ds(first(i), n), :], preferred_element_type=F32)
        out = (acc[:HEAD_DIM] / acc[HEAD_DIM:HEAD_DIM + 1]).T
        rows = pl.ds(i * blk, blk)
        o_ref[rows, :] = (out * _silu(g_ref[rows, :])).astype(o_ref.dtype)


def _moba_prompt(proj, cos, sin, n_seq, n_heads, d_main, gate_off):
    t = proj.shape[0] // n_seq
    assert t % MOBA_BLOCK == 0
    nblk = t // MOBA_BLOCK
    hb = d_main // HEAD_DIM
    col = lambda off: (lambda b, h: (b, off + h))
    tab = pl.BlockSpec((t, HEAD_DIM), lambda b, h: (0, 0))
    assert n_heads % SUBLANES == 0
    cache_rows = pl.BlockSpec((None, t, None, SUBLANES, HEAD_DIM), lambda b, h: (b, 0, h // SUBLANES, 0, 0))
    cache_shape = jax.ShapeDtypeStruct((n_seq, t, n_heads // SUBLANES, SUBLANES, HEAD_DIM), F32)
    return pl.pallas_call(
        _moba_body,
        grid=(n_seq, n_heads),
        in_specs=[pl.BlockSpec((t, HEAD_DIM), col(0)),
                  pl.BlockSpec((t, HEAD_DIM), col(hb)),
                  pl.BlockSpec((t, HEAD_DIM), col(2 * hb)),
                  pl.BlockSpec((t, HEAD_DIM), col(gate_off // HEAD_DIM)),
                  tab, tab],
        out_specs=[cache_rows, cache_rows, pl.BlockSpec((t, HEAD_DIM), lambda b, h: (b, h))],
        out_shape=[cache_shape, cache_shape, jax.ShapeDtypeStruct((n_seq * t, d_main), BF16)],
        scratch_shapes=[pltpu.VMEM((nblk, MOBA_BLOCK, HEAD_DIM), F32),
                        pltpu.VMEM((nblk, MOBA_BLOCK, HEAD_DIM), BF16),
                        pltpu.VMEM((t, HEAD_DIM), BF16),
                        pltpu.VMEM((HEAD_DIM + 2 * SUBLANES, t), BF16),
                        pltpu.VMEM((-(-nblk // SUBLANES) * SUBLANES, HEAD_DIM), F32),
                        pltpu.VMEM((nblk * (nblk + 1) // 2 * MOBA_BLOCK, MOBA_BLOCK), F32),
                        pltpu.VMEM((nblk * (nblk + 1) // 2 * MOBA_BLOCK, MOBA_BLOCK), BF16)],
        compiler_params=_params("arbitrary", "arbitrary"),
        name="moba_prompt",
    )(proj, proj, proj, proj, cos, sin)


def _moba_parts_body(pt_ref, k0_ref, k1_ref, v0_ref, v1_ref, q_ref, kn_ref, vn_ref, cos_ref, sin_ref,
                     po_ref, pm_ref, pl_ref, pg_ref, knew_ref, s_s, qk_s, *, n_heads, n_new, n_past_blocks):
    del pt_ref
    j = pl.program_id(1)
    scale = HEAD_DIM ** -0.5
    cos, sin = cos_ref[...], sin_ref[...]
    rows = q_ref.shape[0]

    def put(ref, h, val):
        ref[:, h * HEAD_DIM:(h + 1) * HEAD_DIM] = jnp.broadcast_to(val, (rows, HEAD_DIM))

    @pl.when(j < n_past_blocks)
    def _():
        for h in range(n_heads):
            hs = slice(h * HEAD_DIM, (h + 1) * HEAD_DIM)
            hr = slice(h * rows, (h + 1) * rows)
            head_rows = pl.ds(h, PAGE_SIZE, stride=n_heads)
            kh = jnp.concatenate([k0_ref[head_rows, :], k1_ref[head_rows, :]], axis=0)
            qh = _rope(q_ref[:, hs], cos, sin)
            s_s[hr, :] = _nt_dot(qh, kh) * scale
            qk_s[hr, :] = qh * (jnp.sum(kh, axis=0, keepdims=True) * (1.0 / MOBA_BLOCK))
        s = s_s[...]
        m = jnp.max(s, axis=1, keepdims=True)
        p = jnp.exp(s - m)
        s_s[...] = p
        lsum = jnp.sum(p, axis=1, keepdims=True)
        gate = jnp.sum(qk_s[...], axis=1, keepdims=True)
        for h in range(n_heads):
            hs = slice(h * HEAD_DIM, (h + 1) * HEAD_DIM)
            hr = slice(h * rows, (h + 1) * rows)
            head_rows = pl.ds(h, PAGE_SIZE, stride=n_heads)
            vh = jnp.concatenate([v0_ref[head_rows, :], v1_ref[head_rows, :]], axis=0)
            po_ref[:, hs] = _nn_dot(s_s[hr, :], vh)
            put(pm_ref, h, m[hr])
            put(pl_ref, h, lsum[hr])
            put(pg_ref, h, gate[hr])

    @pl.when(j == n_past_blocks)
    def _():
        t_i = lax.broadcasted_iota(jnp.int32, (rows, 1), 0)
        for h in range(n_heads):
            hs = slice(h * HEAD_DIM, (h + 1) * HEAD_DIM)
            qh = _rope(q_ref[:, hs], cos, sin)
            kh = _rope(kn_ref[:, hs], cos, sin)
            vh = vn_ref[:, hs]
            knew_ref[:, hs] = kh
            logit = [jnp.where(t_i >= s, jnp.sum(qh * kh[s:s + 1, :], axis=1, keepdims=True) * scale, -jnp.inf)
                     for s in range(n_new)]
            m = functools.reduce(jnp.maximum, logit)
            ps = [jnp.exp(lg - m) for lg in logit]
            po_ref[:, hs] = functools.reduce(lambda a, b: a + b, [p * vh[s:s + 1, :] for s, p in enumerate(ps)])
            put(pm_ref, h, m)
            put(pl_ref, h, functools.reduce(lambda a, b: a + b, ps))
            put(pg_ref, h, jnp.zeros((rows, 1), F32))


def _moba_combine_body(po_ref, pm_ref, pl_ref, pg_ref, g_ref, o_ref, cur_s, sel_s, *, n_past_blocks):
    nb = n_past_blocks
    shape = cur_s.shape
    jidx = lax.broadcasted_iota(jnp.int32, shape, 0)
    cur_s[...] = pg_ref[:nb]
    sel_s[...] = jnp.zeros(shape, F32)
    for _ in range(MOBA_TOPK):
        cur = cur_s[...]
        best = jnp.max(cur, axis=0, keepdims=True)
        first = jnp.min(jnp.where(cur == best, jidx, nb), axis=0, keepdims=True)
        pick = jidx == first
        sel_s[...] = jnp.where(pick & (jnp.abs(best) < jnp.inf), 1.0, sel_s[...])
        cur_s[...] = jnp.where(pick, -jnp.inf, cur)
    sel = sel_s[...] > 0.0
    m_own = pm_ref[nb]
    m_all = jnp.maximum(m_own, jnp.max(jnp.where(sel, pm_ref[:nb], -jnp.inf), axis=0))
    w = jnp.where(sel, jnp.exp(pm_ref[:nb] - m_all[None]), 0.0)
    w_own = jnp.exp(m_own - m_all)
    den = jnp.sum(w * pl_ref[:nb], axis=0) + w_own * pl_ref[nb]
    num = jnp.sum(w * po_ref[:nb], axis=0) + w_own * po_ref[nb]
    o_ref[...] = ((num / den) * _silu(g_ref[...])).astype(o_ref.dtype)


def _moba_decode(proj, k_cache, v_cache, page_table, layer_j, cos, sin, n_seq, n_heads, d_main, gate_off, n_new):
    rows = SAMPLE_ROWS
    n_pages = page_table.shape[1]
    pages_per_block = MOBA_BLOCK // PAGE_SIZE
    assert pages_per_block == 2 and n_pages % pages_per_block == 0
    nb = n_pages // pages_per_block
    page_rows = PAGE_SIZE * n_heads

    def page(p):
        def index(b, j, pt):
            return (layer_j, pt[b, jnp.minimum(pages_per_block * j + p, n_pages - 1)], 0, 0)
        return pl.BlockSpec((None, None, page_rows, HEAD_DIM), index)

    tok = lambda off: pl.BlockSpec((rows, d_main), lambda b, j, pt: (b, off))
    tab = pl.BlockSpec((rows, HEAD_DIM), lambda b, j, pt: (0, 0))
    part = pl.BlockSpec((None, None, rows, d_main), lambda b, j, pt: (b, j, 0, 0))
    part_shape = jax.ShapeDtypeStruct((n_seq, nb + 1, rows, d_main), F32)
    po, pm, pls, pg, k_new = pl.pallas_call(
        functools.partial(_moba_parts_body, n_heads=n_heads, n_new=n_new, n_past_blocks=nb),
        grid_spec=pltpu.PrefetchScalarGridSpec(
            num_scalar_prefetch=1,
            grid=(n_seq, nb + 1),
            in_specs=[page(0), page(1), page(0), page(1), tok(0), tok(1), tok(2), tab, tab],
            out_specs=[part, part, part, part,
                       pl.BlockSpec((rows, d_main), lambda b, j, pt: (b, 0))],
            scratch_shapes=[pltpu.VMEM((n_heads * rows, MOBA_BLOCK), F32),
                            pltpu.VMEM((n_heads * rows, HEAD_DIM), F32)]),
        out_shape=[part_shape, part_shape, part_shape, part_shape,
                   jax.ShapeDtypeStruct((n_seq * rows, d_main), F32)],
        compiler_params=_params("arbitrary", "arbitrary"),
        name="moba_decode_parts",
    )(page_table, k_cache, k_cache, v_cache, v_cache, proj, proj, proj, cos, sin)

    whole = pl.BlockSpec((None, nb + 1, rows, d_main), lambda b: (b, 0, 0, 0))
    gate_blk = 1024
    assert gate_off % gate_blk == 0 and d_main % gate_blk == 0
    n_gate = d_main // gate_blk
    gate_specs = [pl.BlockSpec((rows, gate_blk), functools.partial(lambda b, i: (b, gate_off // gate_blk + i), i=i))
                  for i in range(n_gate)]

    def body(po_ref, pm_ref, pl_ref, pg_ref, *rest):
        g_refs, (o_ref, cur_s, sel_s, g_s) = rest[:n_gate], rest[n_gate:]
        for i, g_ref in enumerate(g_refs):
            g_s[:, i * gate_blk:(i + 1) * gate_blk] = g_ref[...]
        _moba_combine_body(po_ref, pm_ref, pl_ref, pg_ref, g_s, o_ref, cur_s, sel_s, n_past_blocks=nb)

    mix = pl.pallas_call(
        body,
        grid=(n_seq,),
        in_specs=[whole, whole, whole, whole] + gate_specs,
        out_specs=pl.BlockSpec((rows, d_main), lambda b: (b, 0)),
        out_shape=jax.ShapeDtypeStruct((n_seq * rows, d_main), BF16),
        scratch_shapes=[pltpu.VMEM((nb, rows, d_main), F32), pltpu.VMEM((nb, rows, d_main), F32),
                        pltpu.VMEM((rows, d_main), F32)],
        compiler_params=_params("arbitrary"),
        name="moba_decode_combine",
    )(po, pm, pls, pg, *([proj] * n_gate))
    return mix, k_new


def _xattn_body(q_ref, k_ref, v_ref, g_ref, o_ref):
    scale = q_ref.shape[-1] ** -0.5
    s = _nt_dot(q_ref[...], k_ref[...]) * scale
    m = jnp.max(s, axis=-1, keepdims=True)
    p = jnp.exp(s - m)
    p = p / jnp.sum(p, axis=-1, keepdims=True)
    o_ref[...] = (_nn_dot(p, v_ref[...]) * _silu(g_ref[...])).astype(o_ref.dtype)


def _cross_attention(proj, mem_k, mem_v, k_spec, v_spec, n_seq, tq, q_off, gate_off, d_cross):
    t = proj.shape[0] // n_seq
    hd = d_cross // N_MEM_HEADS
    nq = t // tq
    row = lambda off: (lambda b, h, i: (b * nq + i, off + h))
    return pl.pallas_call(
        _xattn_body,
        grid=(n_seq, N_MEM_HEADS, nq),
        in_specs=[pl.BlockSpec((tq, hd), row(q_off // hd)),
                  k_spec, v_spec,
                  pl.BlockSpec((tq, hd), row(gate_off // hd))],
        out_specs=pl.BlockSpec((tq, hd), row(0)),
        out_shape=jax.ShapeDtypeStruct((n_seq * t, d_cross), BF16),
        compiler_params=_params("arbitrary", "arbitrary", "arbitrary"),
        name="mem_xattn",
    )(proj, mem_k, mem_v, proj)


def kernel(x_prompt, x_sample, cache_moba_k, cache_moba_v, state_hgrn, cache_mem_k, cache_mem_v,
           page_table, mem_prompt, norm_g, w_in, w_out, hgrn_lb_logits, hgrn_out_g, mem_norm_g,
           w_mem_kv, final_norm_g):
    bp, tp, d_model = x_prompt.shape
    bs, ts, _ = x_sample.shape
    depth = w_in.shape[0]
    d_mix = w_out.shape[1]
    d_cross = w_mem_kv.shape[2] // 2
    d_main = d_mix - d_cross
    n_heads = d_main // HEAD_DIM
    mem_len = mem_prompt.shape[1]
    mem_hd = d_cross // N_MEM_HEADS
    n_pages = page_table.shape[1]
    past_len = n_pages * PAGE_SIZE
    q_off = 3 * d_main
    gate_off = 3 * d_main + d_cross
    rows = SAMPLE_ROWS
    assert ts <= rows

    gains = norm_g.reshape(depth, 1, d_model)
    mem_gains = mem_norm_g.reshape(depth, 1, d_model)
    out_gains = hgrn_out_g.reshape(-1, 1, HEAD_DIM)

    xp = x_prompt.reshape(bp * tp, d_model)
    xs = jnp.pad(x_sample, ((0, 0), (0, rows - ts), (0, 0))).reshape(bs * rows, d_model)
    mem = mem_prompt.reshape(bp * mem_len, d_model)

    cos_p, sin_p = _rope_tables(jnp.arange(tp, dtype=jnp.int32))
    cos_s, sin_s = _rope_tables(past_len + jnp.arange(rows, dtype=jnp.int32))
    kc = cache_moba_k.reshape(cache_moba_k.shape[0], cache_moba_k.shape[1], PAGE_SIZE * n_heads, HEAD_DIM)
    vc = cache_moba_v.reshape(kc.shape)
    mem_kc = cache_mem_k.reshape(depth, bs, mem_len, d_cross)
    mem_vc = cache_mem_v.reshape(depth, bs, mem_len, d_cross)

    tm = PROJ_ROWS
    outs = {k: [] for k in ("moba_k_p", "moba_v_p", "hgrn_p", "memk_p", "memv_p", "moba_k_s", "moba_v_s", "hgrn_s")}
    for l in range(depth):
        j = l // N_MIXERS
        proj_p, proj_s = _project(_rmsnorm(xp, gains, l, BF16, NORM_ROWS), w_in, l, tm, PROJ_COLS,
                                  extra=_rmsnorm(xs, gains, l, BF16, bs * rows))
        if l % N_MIXERS == 0:
            mix_p, st_p = _hgrn(proj_p, hgrn_lb_logits, out_gains, j, bp, n_heads, d_main, gate_off, None, GLA_CHUNK)
            mix_s, st_s = _hgrn(proj_s, hgrn_lb_logits, out_gains, j, bs, n_heads, d_main, gate_off, state_hgrn, ts)
            outs["hgrn_p"].append(st_p)
            outs["hgrn_s"].append(st_s)
        else:
            k_p, v_p, mix_p = _moba_prompt(proj_p, cos_p, sin_p, bp, n_heads, d_main, gate_off)
            mix_s, k_s = _moba_decode(proj_s, kc, vc, page_table, j, cos_s, sin_s, bs, n_heads, d_main, gate_off, ts)
            outs["moba_k_p"].append(k_p.reshape(bp, tp, n_heads, HEAD_DIM))
            outs["moba_v_p"].append(v_p.reshape(bp, tp, n_heads, HEAD_DIM))
            outs["moba_k_s"].append(k_s.reshape(bs, rows, n_heads, HEAD_DIM)[:, :ts])
            outs["moba_v_s"].append(proj_s[:, 2 * d_main:3 * d_main].reshape(bs, rows, n_heads, HEAD_DIM)[:, :ts])
        kv = _project(_rmsnorm(mem, mem_gains, l, BF16, NORM_ROWS), w_mem_kv, l, MEM_PROJ_ROWS, PROJ_COLS)
        outs["memk_p"].append(kv[:, :d_cross].reshape(bp, mem_len, N_MEM_HEADS, mem_hd))
        outs["memv_p"].append(kv[:, d_cross:].reshape(bp, mem_len, N_MEM_HEADS, mem_hd))
        cross_p = _cross_attention(
            proj_p, kv, kv,
            pl.BlockSpec((mem_len, mem_hd), lambda b, h, i: (b, h)),
            pl.BlockSpec((mem_len, mem_hd), lambda b, h, i: (b, N_MEM_HEADS + h)),
            bp, XATTN_ROWS, q_off, gate_off + d_main, d_cross)
        cached = pl.BlockSpec((None, None, mem_len, mem_hd), functools.partial(lambda b, h, i, l: (l, b, 0, h), l=l))
        cross_s = _cross_attention(proj_s, mem_kc, mem_vc, cached, cached, bs, rows, q_off, gate_off + d_main, d_cross)
        xp, xs = _out_project(mix_p, cross_p, w_out, l, xp, OUT_PROJ_ROWS, OUT_PROJ_COLS, extra=(mix_s, cross_s, xs))

    final_g = final_norm_g.reshape(1, 1, d_model)
    y_prompt = _rmsnorm(xp, final_g, 0, F32, NORM_ROWS).reshape(bp, tp, d_model)
    y_sample = _rmsnorm(xs, final_g, 0, F32, bs * rows).reshape(bs, rows, d_model)[:, :ts]
    return (y_prompt, y_sample,
            jnp.stack(outs["moba_k_p"]), jnp.stack(outs["moba_v_p"]), jnp.stack(outs["hgrn_p"]),
            jnp.stack(outs["memk_p"]), jnp.stack(outs["memv_p"]),
            jnp.stack(outs["moba_k_s"]), jnp.stack(outs["moba_v_s"]), jnp.stack(outs["hgrn_s"]))
```

```python
import functools

import jax
import jax.numpy as jnp
from jax import lax
from jax.experimental import pallas as pl
from jax.experimental.pallas import tpu as pltpu

F32 = jnp.float32
BF16 = jnp.bfloat16

HEAD_DIM = 128
N_MEM_HEADS = 4
PAGE_SIZE = 128
MOBA_BLOCK = 256
MOBA_TOPK = 3
N_MIXERS = 2
ROPE_THETA = 10000.0
RMS_EPS = 1e-6
MIN_FORGET = 1e-30
LOG2_E = 1.4426950408889634

LANES = 128
SUBLANES = 8
VMEM_LIMIT_BYTES = 52 * 1024 * 1024

PROJ_ROWS = 512
OUT_PROJ_ROWS = 512
MEM_PROJ_ROWS = 512
PROJ_COLS = 1024
OUT_PROJ_COLS = 1024
NORM_ROWS = 256
XATTN_ROWS = 1024
GLA_CHUNK = 128
SAMPLE_ROWS = SUBLANES


def _params(*semantics):
    return pltpu.CompilerParams(dimension_semantics=semantics, vmem_limit_bytes=VMEM_LIMIT_BYTES)


def _nt_dot(a, b):
    return lax.dot_general(a.astype(BF16), b.astype(BF16), (((1,), (1,)), ((), ())),
                           preferred_element_type=F32)


def _nn_dot(a, b):
    return jnp.dot(a.astype(BF16), b.astype(BF16), preferred_element_type=F32)


def _silu(x):
    return x * jax.nn.sigmoid(x)


def _rmsnorm_body(x_ref, g_ref, o_ref):
    x = x_ref[...]
    inv = lax.rsqrt(jnp.mean(x * x, axis=-1, keepdims=True) + RMS_EPS)
    o_ref[...] = (x * inv * g_ref[...]).astype(o_ref.dtype)


def _rmsnorm(x, gains, layer, out_dtype, tm):
    m, d = x.shape
    return pl.pallas_call(
        _rmsnorm_body,
        grid=(m // tm,),
        in_specs=[pl.BlockSpec((tm, d), lambda i: (i, 0)),
                  pl.BlockSpec((None, 1, d), lambda i: (layer, 0, 0))],
        out_specs=pl.BlockSpec((tm, d), lambda i: (i, 0)),
        out_shape=jax.ShapeDtypeStruct((m, d), out_dtype),
        compiler_params=_params("arbitrary"),
        name="rmsnorm",
    )(x, gains)


def _stage_weights(w_hbm, stage_ref, wb_ref, sem, layer, tn, on_new_block=None):
    j, i = pl.program_id(0), pl.program_id(1)
    n_slots = stage_ref.shape[0]

    def fetch(jj):
        cols = pl.ds(pl.multiple_of(jj * tn, tn), tn)
        slot = jj % n_slots
        return pltpu.make_async_copy(w_hbm.at[layer, :, cols], stage_ref.at[slot], sem.at[slot])

    def prefetch_next():
        @pl.when(j + 1 < pl.num_programs(0))
        def _():
            fetch(j + 1).start()

    @pl.when(i == 0)
    def _():
        @pl.when(j == 0)
        def _():
            fetch(0).start()

        if n_slots > 1:
            prefetch_next()
        fetch(j).wait()
        wb_ref[...] = stage_ref[j % n_slots].astype(BF16)
        if n_slots == 1:
            prefetch_next()
        if on_new_block is not None:
            on_new_block()


def _weight_scratch(kdim, tn, n_row_tiles):
    n_slots = 2 if n_row_tiles == 1 else 1
    return [pltpu.VMEM((n_slots, kdim, tn), F32), pltpu.VMEM((kdim, tn), BF16), pltpu.SemaphoreType.DMA((n_slots,))]


def _proj_body(*refs, layer, tn, with_extra):
    if with_extra:
        x_ref, xe_ref, w_hbm, o_ref, oe_ref, stage_ref, wb_ref, sem = refs
    else:
        x_ref, w_hbm, o_ref, stage_ref, wb_ref, sem = refs

    def extra_rows():
        oe_ref[...] = jnp.dot(xe_ref[...], wb_ref[...], preferred_element_type=F32)

    _stage_weights(w_hbm, stage_ref, wb_ref, sem, layer, tn, extra_rows if with_extra else None)
    o_ref[...] = jnp.dot(x_ref[...], wb_ref[...], preferred_element_type=F32)


def _project(x, w, layer, tm, tn, extra=None):
    m, kdim = x.shape
    n = w.shape[-1]
    in_specs = [pl.BlockSpec((tm, kdim), lambda j, i: (i, 0))]
    out_specs = [pl.BlockSpec((tm, tn), lambda j, i: (i, j))]
    out_shape = [jax.ShapeDtypeStruct((m, n), F32)]
    args = [x]
    if extra is not None:
        me = extra.shape[0]
        in_specs.append(pl.BlockSpec((me, kdim), lambda j, i: (0, 0)))
        out_specs.append(pl.BlockSpec((me, tn), lambda j, i: (0, j)))
        out_shape.append(jax.ShapeDtypeStruct((me, n), F32))
        args.append(extra)
    outs = pl.pallas_call(
        functools.partial(_proj_body, layer=layer, tn=tn, with_extra=extra is not None),
        grid=(n // tn, m // tm),
        in_specs=in_specs + [pl.BlockSpec(memory_space=pl.ANY)],
        out_specs=out_specs,
        out_shape=out_shape,
        scratch_shapes=_weight_scratch(kdim, tn, m // tm),
        compiler_params=_params("arbitrary", "arbitrary"),
        name="in_proj",
    )(*args, w)
    return outs if extra is not None else outs[0]


def _outproj_body(*refs, d_main, layer, tn, with_extra):
    if with_extra:
        a_ref, c_ref, r_ref, ae_ref, ce_ref, re_ref, w_hbm, o_ref, oe_ref, stage_ref, wb_ref, sem = refs
    else:
        a_ref, c_ref, r_ref, w_hbm, o_ref, stage_ref, wb_ref, sem = refs

    def rows(a, c, r):
        acc = jnp.dot(a[...], wb_ref[:d_main, :], preferred_element_type=F32)
        acc = acc + jnp.dot(c[...], wb_ref[d_main:, :], preferred_element_type=F32)
        return r[...] + acc

    def extra_rows():
        oe_ref[...] = rows(ae_ref, ce_ref, re_ref)

    _stage_weights(w_hbm, stage_ref, wb_ref, sem, layer, tn, extra_rows if with_extra else None)
    o_ref[...] = rows(a_ref, c_ref, r_ref)


def _out_project(mix_main, mix_cross, w, layer, resid, tm, tn, extra=None):
    m, d_main = mix_main.shape
    d_cross = mix_cross.shape[1]
    kdim, n = w.shape[1], w.shape[2]

    def specs(rows_per_step, row_index):
        return [pl.BlockSpec((rows_per_step, d_main), lambda j, i: (row_index(i), 0)),
                pl.BlockSpec((rows_per_step, d_cross), lambda j, i: (row_index(i), 0)),
                pl.BlockSpec((rows_per_step, tn), lambda j, i: (row_index(i), j))]

    in_specs = specs(tm, lambda i: i)
    out_specs = [pl.BlockSpec((tm, tn), lambda j, i: (i, j))]
    out_shape = [jax.ShapeDtypeStruct((m, n), F32)]
    args = [mix_main, mix_cross, resid]
    if extra is not None:
        me = extra[0].shape[0]
        in_specs += specs(me, lambda i: 0)
        out_specs.append(pl.BlockSpec((me, tn), lambda j, i: (0, j)))
        out_shape.append(jax.ShapeDtypeStruct((me, n), F32))
        args += list(extra)
    outs = pl.pallas_call(
        functools.partial(_outproj_body, d_main=d_main, layer=layer, tn=tn, with_extra=extra is not None),
        grid=(n // tn, m // tm),
        in_specs=in_specs + [pl.BlockSpec(memory_space=pl.ANY)],
        out_specs=out_specs,
        out_shape=out_shape,
        scratch_shapes=_weight_scratch(kdim, tn, m // tm),
        compiler_params=_params("arbitrary", "arbitrary"),
        name="out_proj",
    )(*args, w)
    return outs if extra is not None else outs[0]


def _hgrn_body(q_ref, f_ref, i_ref, g_ref, lbl_ref, og_ref, *rest, layer_j, n_valid, has_s0):
    rest = list(rest)
    s0_ref = rest.pop(0) if has_s0 else None
    o_ref, sfin_ref, mask_s = rest
    c = GLA_CHUNK
    n_rows = q_ref.shape[0]

    n_layers = lbl_ref.shape[0]
    logits = [lbl_ref[i:i + 1, :] for i in range(n_layers)]
    mx = functools.reduce(jnp.maximum, logits)
    es = [jnp.exp(l - mx) for l in logits]
    tot = functools.reduce(lambda a, b: a + b, es)
    ws = [e / tot for e in es]
    csum = ws[0]
    for i in range(1, layer_j + 1):
        csum = csum + ws[i]
    lb = csum - ws[0]
    og = og_ref[...]

    row = lax.broadcasted_iota(jnp.int32, (c, HEAD_DIM), 0)
    tt = lax.broadcasted_iota(jnp.int32, (c, c), 0)
    ss = lax.broadcasted_iota(jnp.int32, (c, c), 1)
    n_levels = (min(n_valid, c) - 1).bit_length()
    for lvl in range(n_levels):
        m = 1 << lvl
        pair = ((tt >> (lvl + 1)) == (ss >> (lvl + 1))) & ((tt & m) != 0) & ((ss & m) == 0)
        mask_s[lvl] = jnp.where(pair, 1.0, 0.0)
    mask_s[n_levels] = jnp.where(tt == ss, 1.0, 0.0)

    def prefix_sum(x):
        s = 1
        while s < c:
            x = x + jnp.where(row >= s, pltpu.roll(x, s, 0), 0.0)
            s *= 2
        return x

    def gates_part(fxc):
        f = lb + (1.0 - lb) * jax.nn.sigmoid(fxc)
        logf = jnp.log(jnp.maximum(f, MIN_FORGET))
        kk = 1.0 - f
        if n_valid < c:
            ok = row < n_valid
            logf = jnp.where(ok, logf, 0.0)
            kk = jnp.where(ok, kk, 0.0)
        return kk, prefix_sum(logf)

    def local_part(qc, kk, cum, vc):
        att = _nt_dot(qc, kk) * mask_s[n_levels]
        end = cum
        for lvl in range(n_levels):
            m = 1 << lvl
            if 2 * m >= SUBLANES:
                g = c // (2 * m)
                ref = jnp.broadcast_to(cum.reshape(g, 2 * m, HEAD_DIM)[:, m - 1:m, :],
                                       (g, 2 * m, HEAD_DIM)).reshape(c, HEAD_DIM)
            else:
                upper = (row & m) != 0
                ref = jnp.where(upper, pltpu.roll(end, m, 0), end)
                end = jnp.where(upper, end, pltpu.roll(end, c - m, 0))
            e = jnp.exp2(jnp.abs(cum - ref) * (-LOG2_E))
            att = att + _nt_dot(qc * e, kk * e) * mask_s[lvl]
        last = cum[c - 1:c, :]
        intra = _nn_dot(att, vc)
        q_dec = (qc * jnp.exp(cum)).astype(BF16)
        kv = _nn_dot(vc.T, kk * jnp.exp(last - cum))
        return intra, q_dec, kv, jnp.exp(last)

    def state_part(intra, q_dec, kv, dec, st, gate):
        o = _nt_dot(q_dec, st) + intra
        y = o * lax.rsqrt(jnp.mean(o * o, axis=-1, keepdims=True) + RMS_EPS) * og
        return y * _silu(gate), st * dec + kv

    st0 = s0_ref[...].T if has_s0 else jnp.zeros((HEAD_DIM, HEAD_DIM), F32)
    if n_rows < c:
        pad = jnp.zeros((c - n_rows, HEAD_DIM), F32)
        grow = lambda ref: jnp.concatenate([ref[...], pad], axis=0)
        y, st = state_part(*local_part(grow(q_ref), *gates_part(grow(f_ref)), grow(i_ref)), st0, grow(g_ref))
        o_ref[...] = y[:n_rows].astype(o_ref.dtype)
    else:
        def body(ci, st):
            rows = pl.ds(pl.multiple_of(ci * c, c), c)
            local = local_part(q_ref[rows, :], *gates_part(f_ref[rows, :]), i_ref[rows, :])
            y, st = state_part(*local, st, g_ref[rows, :])
            o_ref[rows, :] = y.astype(o_ref.dtype)
            return st

        st = lax.fori_loop(0, n_rows // c, body, st0, unroll=True)
    sfin_ref[...] = st.T


def _hgrn(proj, lb_logits, out_g, layer_j, n_seq, n_heads, d_main, gate_off, s0, n_valid):
    t = proj.shape[0] // n_seq
    hb = d_main // HEAD_DIM
    col = lambda off: (lambda b, h: (b, off + h))
    in_specs = [pl.BlockSpec((t, HEAD_DIM), col(0)),
                pl.BlockSpec((t, HEAD_DIM), col(hb)),
                pl.BlockSpec((t, HEAD_DIM), col(2 * hb)),
                pl.BlockSpec((t, HEAD_DIM), col(gate_off // HEAD_DIM)),
                pl.BlockSpec((lb_logits.shape[0], HEAD_DIM), lambda b, h: (0, h)),
                pl.BlockSpec((None, 1, HEAD_DIM), lambda b, h: (layer_j, 0, 0))]
    args = [proj, proj, proj, proj, lb_logits, out_g]
    if s0 is not None:
        in_specs.append(pl.BlockSpec((None, None, None, HEAD_DIM, HEAD_DIM),
                                     lambda b, h: (layer_j, b, h, 0, 0)))
        args.append(s0)
    return pl.pallas_call(
        functools.partial(_hgrn_body, layer_j=layer_j, n_valid=n_valid, has_s0=s0 is not None),
        grid=(n_seq, n_heads),
        in_specs=in_specs,
        out_specs=[pl.BlockSpec((t, HEAD_DIM), lambda b, h: (b, h)),
                   pl.BlockSpec((None, None, HEAD_DIM, HEAD_DIM), lambda b, h: (b, h, 0, 0))],
        out_shape=[jax.ShapeDtypeStruct((n_seq * t, d_main), BF16),
                   jax.ShapeDtypeStruct((n_seq, n_heads, HEAD_DIM, HEAD_DIM), F32)],
        scratch_shapes=[pltpu.VMEM((GLA_CHUNK.bit_length(), GLA_CHUNK, GLA_CHUNK), F32)],
        compiler_params=_params("arbitrary", "arbitrary"),
        name="hgrn2",
    )(*args)


def _rope_tables(pos):
    half = HEAD_DIM // 2
    inv = ROPE_THETA ** (-jnp.arange(half, dtype=F32) / half)
    ang = pos.astype(F32)[:, None] * inv[None, :]
    cos, sin = jnp.cos(ang), jnp.sin(ang)
    return jnp.concatenate([cos, cos], axis=-1), jnp.concatenate([-sin, sin], axis=-1)


def _rope(x, cos, sin):
    return x * cos + pltpu.roll(x, HEAD_DIM // 2, 1) * sin


def _moba_body(q_ref, k_ref, v_ref, g_ref, cos_ref, sin_ref, kout_ref, vout_ref, o_ref,
               qf_s, qb_s, kb_s, vt_s, km_s, s_s, p_s):
    blk = MOBA_BLOCK
    nblk = q_ref.shape[0] // blk
    scale = HEAD_DIM ** -0.5

    t = q_ref.shape[0]
    head_slot = pl.program_id(1) % SUBLANES
    kout_rows = kout_ref.reshape(t * SUBLANES, HEAD_DIM)
    vout_rows = vout_ref.reshape(t * SUBLANES, HEAD_DIM)

    km_s[...] = jnp.zeros_like(km_s)
    for j in range(nblk):
        rows = pl.ds(j * blk, blk)
        head_rows = pl.ds(j * blk * SUBLANES + head_slot, blk, stride=SUBLANES)
        cos, sin = cos_ref[rows, :], sin_ref[rows, :]
        q = _rope(q_ref[rows, :], cos, sin)
        k = _rope(k_ref[rows, :], cos, sin)
        v = v_ref[rows, :]
        kout_rows[head_rows, :] = k
        vout_rows[head_rows, :] = v
        qf_s[j] = q
        qb_s[j] = q.astype(BF16)
        kb_s[rows, :] = k.astype(BF16)
        vt_s[:HEAD_DIM, rows] = v.T.astype(BF16)
        vt_s[HEAD_DIM:, rows] = jnp.ones((vt_s.shape[0] - HEAD_DIM, blk), BF16)
        km_s[j:j + 1, :] = jnp.sum(k, axis=0, keepdims=True) * (1.0 / blk)

    npad = km_s.shape[0]
    jidx = lax.broadcasted_iota(jnp.int32, (npad, blk), 0)
    causal = (lax.broadcasted_iota(jnp.int32, (blk, blk), 0) <=
              lax.broadcasted_iota(jnp.int32, (blk, blk), 1))

    first = lambda i: (i * (i + 1) // 2) * blk
    col_max = []
    for i in range(nblk):
        gate = lax.dot_general(km_s[...], qf_s[i], (((1,), (1,)), ((), ())),
                               precision=lax.Precision.HIGHEST, preferred_element_type=F32)
        cand = jidx < i
        gm = jnp.where(cand, gate, -jnp.inf)
        rank = jnp.zeros((npad, blk), jnp.int32)
        for jp in range(i):
            gj = gm[jp:jp + 1, :]
            rank = rank + jnp.where((gj > gm) | ((gj == gm) & (jp < jidx)), 1, 0)
        sel = jnp.where(cand & (rank < MOBA_TOPK) & (jnp.abs(gate) < jnp.inf), 1.0, 0.0)

        m = jnp.full((1, blk), -jnp.inf, F32)
        qi = qb_s[i]
        for j in range(i + 1):
            mask = causal if j == i else jnp.broadcast_to(sel[j:j + 1, :], (blk, blk)) > 0.0
            st = jnp.where(mask, _nt_dot(kb_s[pl.ds(j * blk, blk), :], qi), -jnp.inf)
            s_s[pl.ds(first(i) + j * blk, blk), :] = st
            m = jnp.maximum(m, jnp.max(st, axis=0, keepdims=True))
        col_max.append(m)
    for i in range(nblk):
        for j in range(i + 1):
            rows = pl.ds(first(i) + j * blk, blk)
            p_s[rows, :] = jnp.exp2((s_s[rows, :] - col_max[i]) * (scale * LOG2_E)).astype(BF16)
    for i in range(nblk):
        n = (i + 1) * blk
        acc = jnp.dot(vt_s[:, :n], p_s[pl.ds(first(i), n), :], preferred_element_type=F32)
        out = (acc[:HEAD_DIM] / acc[HEAD_DIM:HEAD_DIM + 1]).T
        rows = pl.ds(i * blk, blk)
        o_ref[rows, :] = (out * _silu(g_ref[rows, :])).astype(o_ref.dtype)


def _moba_prompt(proj, cos, sin, n_seq, n_heads, d_main, gate_off):
    t = proj.shape[0] // n_seq
    assert t % MOBA_BLOCK == 0
    nblk = t // MOBA_BLOCK
    hb = d_main // HEAD_DIM
    col = lambda off: (lambda b, h: (b, off + h))
    tab = pl.BlockSpec((t, HEAD_DIM), lambda b, h: (0, 0))
    assert n_heads % SUBLANES == 0
    cache_rows = pl.BlockSpec((None, t, None, SUBLANES, HEAD_DIM), lambda b, h: (b, 0, h // SUBLANES, 0, 0))
    cache_shape = jax.ShapeDtypeStruct((n_seq, t, n_heads // SUBLANES, SUBLANES, HEAD_DIM), F32)
    return pl.pallas_call(
        _moba_body,
        grid=(n_seq, n_heads),
        in_specs=[pl.BlockSpec((t, HEAD_DIM), col(0)),
                  pl.BlockSpec((t, HEAD_DIM), col(hb)),
                  pl.BlockSpec((t, HEAD_DIM), col(2 * hb)),
                  pl.BlockSpec((t, HEAD_DIM), col(gate_off // HEAD_DIM)),
                  tab, tab],
        out_specs=[cache_rows, cache_rows, pl.BlockSpec((t, HEAD_DIM), lambda b, h: (b, h))],
        out_shape=[cache_shape, cache_shape, jax.ShapeDtypeStruct((n_seq * t, d_main), BF16)],
        scratch_shapes=[pltpu.VMEM((nblk, MOBA_BLOCK, HEAD_DIM), F32),
                        pltpu.VMEM((nblk, MOBA_BLOCK, HEAD_DIM), BF16),
                        pltpu.VMEM((t, HEAD_DIM), BF16),
                        pltpu.VMEM((HEAD_DIM + 2 * SUBLANES, t), BF16),
                        pltpu.VMEM((-(-nblk // SUBLANES) * SUBLANES, HEAD_DIM), F32),
                        pltpu.VMEM((nblk * (nblk + 1) // 2 * MOBA_BLOCK, MOBA_BLOCK), F32),
                        pltpu.VMEM((nblk * (nblk + 1) // 2 * MOBA_BLOCK, MOBA_BLOCK), BF16)],
        compiler_params=_params("arbitrary", "arbitrary"),
        name="moba_prompt",
    )(proj, proj, proj, proj, cos, sin)


def _moba_parts_body(pt_ref, k0_ref, k1_ref, v0_ref, v1_ref, q_ref, kn_ref, vn_ref, cos_ref, sin_ref,
                     po_ref, pm_ref, pl_ref, pg_ref, knew_ref, s_s, qk_s, *, n_heads, n_new, n_past_blocks):
    del pt_ref
    j = pl.program_id(1)
    scale = HEAD_DIM ** -0.5
    cos, sin = cos_ref[...], sin_ref[...]
    rows = q_ref.shape[0]

    def put(ref, h, val):
        ref[:, h * HEAD_DIM:(h + 1) * HEAD_DIM] = jnp.broadcast_to(val, (rows, HEAD_DIM))

    @pl.when(j < n_past_blocks)
    def _():
        for h in range(n_heads):
            hs = slice(h * HEAD_DIM, (h + 1) * HEAD_DIM)
            hr = slice(h * rows, (h + 1) * rows)
            head_rows = pl.ds(h, PAGE_SIZE, stride=n_heads)
            kh = jnp.concatenate([k0_ref[head_rows, :], k1_ref[head_rows, :]], axis=0)
            qh = _rope(q_ref[:, hs], cos, sin)
            s_s[hr, :] = _nt_dot(qh, kh) * scale
            qk_s[hr, :] = qh * (jnp.sum(kh, axis=0, keepdims=True) * (1.0 / MOBA_BLOCK))
        s = s_s[...]
        m = jnp.max(s, axis=1, keepdims=True)
        p = jnp.exp(s - m)
        s_s[...] = p
        lsum = jnp.sum(p, axis=1, keepdims=True)
        gate = jnp.sum(qk_s[...], axis=1, keepdims=True)
        for h in range(n_heads):
            hs = slice(h * HEAD_DIM, (h + 1) * HEAD_DIM)
            hr = slice(h * rows, (h + 1) * rows)
            head_rows = pl.ds(h, PAGE_SIZE, stride=n_heads)
            vh = jnp.concatenate([v0_ref[head_rows, :], v1_ref[head_rows, :]], axis=0)
            po_ref[:, hs] = _nn_dot(s_s[hr, :], vh)
            put(pm_ref, h, m[hr])
            put(pl_ref, h, lsum[hr])
            put(pg_ref, h, gate[hr])

    @pl.when(j == n_past_blocks)
    def _():
        t_i = lax.broadcasted_iota(jnp.int32, (rows, 1), 0)
        for h in range(n_heads):
            hs = slice(h * HEAD_DIM, (h + 1) * HEAD_DIM)
            qh = _rope(q_ref[:, hs], cos, sin)
            kh = _rope(kn_ref[:, hs], cos, sin)
            vh = vn_ref[:, hs]
            knew_ref[:, hs] = kh
            logit = [jnp.where(t_i >= s, jnp.sum(qh * kh[s:s + 1, :], axis=1, keepdims=True) * scale, -jnp.inf)
                     for s in range(n_new)]
            m = functools.reduce(jnp.maximum, logit)
            ps = [jnp.exp(lg - m) for lg in logit]
            po_ref[:, hs] = functools.reduce(lambda a, b: a + b, [p * vh[s:s + 1, :] for s, p in enumerate(ps)])
            put(pm_ref, h, m)
            put(pl_ref, h, functools.reduce(lambda a, b: a + b, ps))
            put(pg_ref, h, jnp.zeros((rows, 1), F32))


def _moba_combine_body(po_ref, pm_ref, pl_ref, pg_ref, g_ref, o_ref, cur_s, sel_s, *, n_past_blocks):
    nb = n_past_blocks
    shape = cur_s.shape
    jidx = lax.broadcasted_iota(jnp.int32, shape, 0)
    cur_s[...] = pg_ref[:nb]
    sel_s[...] = jnp.zeros(shape, F32)
    for _ in range(MOBA_TOPK):
        cur = cur_s[...]
        best = jnp.max(cur, axis=0, keepdims=True)
        first = jnp.min(jnp.where(cur == best, jidx, nb), axis=0, keepdims=True)
        pick = jidx == first
        sel_s[...] = jnp.where(pick & (jnp.abs(best) < jnp.inf), 1.0, sel_s[...])
        cur_s[...] = jnp.where(pick, -jnp.inf, cur)
    sel = sel_s[...] > 0.0
    m_own = pm_ref[nb]
    m_all = jnp.maximum(m_own, jnp.max(jnp.where(sel, pm_ref[:nb], -jnp.inf), axis=0))
    w = jnp.where(sel, jnp.exp(pm_ref[:nb] - m_all[None]), 0.0)
    w_own = jnp.exp(m_own - m_all)
    den = jnp.sum(w * pl_ref[:nb], axis=0) + w_own * pl_ref[nb]
    num = jnp.sum(w * po_ref[:nb], axis=0) + w_own * po_ref[nb]
    o_ref[...] = ((num / den) * _silu(g_ref[...])).astype(o_ref.dtype)


def _moba_decode(proj, k_cache, v_cache, page_table, layer_j, cos, sin, n_seq, n_heads, d_main, gate_off, n_new):
    rows = SAMPLE_ROWS
    n_pages = page_table.shape[1]
    pages_per_block = MOBA_BLOCK // PAGE_SIZE
    assert pages_per_block == 2 and n_pages % pages_per_block == 0
    nb = n_pages // pages_per_block
    page_rows = PAGE_SIZE * n_heads

    def page(p):
        def index(b, j, pt):
            return (layer_j, pt[b, jnp.minimum(pages_per_block * j + p, n_pages - 1)], 0, 0)
        return pl.BlockSpec((None, None, page_rows, HEAD_DIM), index)

    tok = lambda off: pl.BlockSpec((rows, d_main), lambda b, j, pt: (b, off))
    tab = pl.BlockSpec((rows, HEAD_DIM), lambda b, j, pt: (0, 0))
    part = pl.BlockSpec((None, None, rows, d_main), lambda b, j, pt: (b, j, 0, 0))
    part_shape = jax.ShapeDtypeStruct((n_seq, nb + 1, rows, d_main), F32)
    po, pm, pls, pg, k_new = pl.pallas_call(
        functools.partial(_moba_parts_body, n_heads=n_heads, n_new=n_new, n_past_blocks=nb),
        grid_spec=pltpu.PrefetchScalarGridSpec(
            num_scalar_prefetch=1,
            grid=(n_seq, nb + 1),
            in_specs=[page(0), page(1), page(0), page(1), tok(0), tok(1), tok(2), tab, tab],
            out_specs=[part, part, part, part,
                       pl.BlockSpec((rows, d_main), lambda b, j, pt: (b, 0))],
            scratch_shapes=[pltpu.VMEM((n_heads * rows, MOBA_BLOCK), F32),
                            pltpu.VMEM((n_heads * rows, HEAD_DIM), F32)]),
        out_shape=[part_shape, part_shape, part_shape, part_shape,
                   jax.ShapeDtypeStruct((n_seq * rows, d_main), F32)],
        compiler_params=_params("arbitrary", "arbitrary"),
        name="moba_decode_parts",
    )(page_table, k_cache, k_cache, v_cache, v_cache, proj, proj, proj, cos, sin)

    whole = pl.BlockSpec((None, nb + 1, rows, d_main), lambda b: (b, 0, 0, 0))
    gate_blk = 1024
    assert gate_off % gate_blk == 0 and d_main % gate_blk == 0
    n_gate = d_main // gate_blk
    gate_specs = [pl.BlockSpec((rows, gate_blk), functools.partial(lambda b, i: (b, gate_off // gate_blk + i), i=i))
                  for i in range(n_gate)]

    def body(po_ref, pm_ref, pl_ref, pg_ref, *rest):
        g_refs, (o_ref, cur_s, sel_s, g_s) = rest[:n_gate], rest[n_gate:]
        for i, g_ref in enumerate(g_refs):
            g_s[:, i * gate_blk:(i + 1) * gate_blk] = g_ref[...]
        _moba_combine_body(po_ref, pm_ref, pl_ref, pg_ref, g_s, o_ref, cur_s, sel_s, n_past_blocks=nb)

    mix = pl.pallas_call(
        body,
        grid=(n_seq,),
        in_specs=[whole, whole, whole, whole] + gate_specs,
        out_specs=pl.BlockSpec((rows, d_main), lambda b: (b, 0)),
        out_shape=jax.ShapeDtypeStruct((n_seq * rows, d_main), BF16),
        scratch_shapes=[pltpu.VMEM((nb, rows, d_main), F32), pltpu.VMEM((nb, rows, d_main), F32),
                        pltpu.VMEM((rows, d_main), F32)],
        compiler_params=_params("arbitrary"),
        name="moba_decode_combine",
    )(po, pm, pls, pg, *([proj] * n_gate))
    return mix, k_new


def _xattn_body(q_ref, k_ref, v_ref, g_ref, o_ref):
    scale = q_ref.shape[-1] ** -0.5
    s = _nt_dot(q_ref[...], k_ref[...]) * scale
    m = jnp.max(s, axis=-1, keepdims=True)
    p = jnp.exp(s - m)
    p = p / jnp.sum(p, axis=-1, keepdims=True)
    o_ref[...] = (_nn_dot(p, v_ref[...]) * _silu(g_ref[...])).astype(o_ref.dtype)


def _cross_attention(proj, mem_k, mem_v, k_spec, v_spec, n_seq, tq, q_off, gate_off, d_cross):
    t = proj.shape[0] // n_seq
    hd = d_cross // N_MEM_HEADS
    nq = t // tq
    row = lambda off: (lambda b, h, i: (b * nq + i, off + h))
    return pl.pallas_call(
        _xattn_body,
        grid=(n_seq, N_MEM_HEADS, nq),
        in_specs=[pl.BlockSpec((tq, hd), row(q_off // hd)),
                  k_spec, v_spec,
                  pl.BlockSpec((tq, hd), row(gate_off // hd))],
        out_specs=pl.BlockSpec((tq, hd), row(0)),
        out_shape=jax.ShapeDtypeStruct((n_seq * t, d_cross), BF16),
        compiler_params=_params("arbitrary", "arbitrary", "arbitrary"),
        name="mem_xattn",
    )(proj, mem_k, mem_v, proj)


def kernel(x_prompt, x_sample, cache_moba_k, cache_moba_v, state_hgrn, cache_mem_k, cache_mem_v,
           page_table, mem_prompt, norm_g, w_in, w_out, hgrn_lb_logits, hgrn_out_g, mem_norm_g,
           w_mem_kv, final_norm_g):
    bp, tp, d_model = x_prompt.shape
    bs, ts, _ = x_sample.shape
    depth = w_in.shape[0]
    d_mix = w_out.shape[1]
    d_cross = w_mem_kv.shape[2] // 2
    d_main = d_mix - d_cross
    n_heads = d_main // HEAD_DIM
    mem_len = mem_prompt.shape[1]
    mem_hd = d_cross // N_MEM_HEADS
    n_pages = page_table.shape[1]
    past_len = n_pages * PAGE_SIZE
    q_off = 3 * d_main
    gate_off = 3 * d_main + d_cross
    rows = SAMPLE_ROWS
    assert ts <= rows

    gains = norm_g.reshape(depth, 1, d_model)
    mem_gains = mem_norm_g.reshape(depth, 1, d_model)
    out_gains = hgrn_out_g.reshape(-1, 1, HEAD_DIM)

    xp = x_prompt.reshape(bp * tp, d_model)
    xs = jnp.pad(x_sample, ((0, 0), (0, rows - ts), (0, 0))).reshape(bs * rows, d_model)
    mem = mem_prompt.reshape(bp * mem_len, d_model)

    cos_p, sin_p = _rope_tables(jnp.arange(tp, dtype=jnp.int32))
    cos_s, sin_s = _rope_tables(past_len + jnp.arange(rows, dtype=jnp.int32))
    kc = cache_moba_k.reshape(cache_moba_k.shape[0], cache_moba_k.shape[1], PAGE_SIZE * n_heads, HEAD_DIM)
    vc = cache_moba_v.reshape(kc.shape)
    mem_kc = cache_mem_k.reshape(depth, bs, mem_len, d_cross)
    mem_vc = cache_mem_v.reshape(depth, bs, mem_len, d_cross)

    tm = PROJ_ROWS
    outs = {k: [] for k in ("moba_k_p", "moba_v_p", "hgrn_p", "memk_p", "memv_p", "moba_k_s", "moba_v_s", "hgrn_s")}
    for l in range(depth):
        j = l // N_MIXERS
        proj_p, proj_s = _project(_rmsnorm(xp, gains, l, BF16, NORM_ROWS), w_in, l, tm, PROJ_COLS,
                                  extra=_rmsnorm(xs, gains, l, BF16, bs * rows))
        if l % N_MIXERS == 0:
            mix_p, st_p = _hgrn(proj_p, hgrn_lb_logits, out_gains, j, bp, n_heads, d_main, gate_off, None, GLA_CHUNK)
            mix_s, st_s = _hgrn(proj_s, hgrn_lb_logits, out_gains, j, bs, n_heads, d_main, gate_off, state_hgrn, ts)
            outs["hgrn_p"].append(st_p)
            outs["hgrn_s"].append(st_s)
        else:
            k_p, v_p, mix_p = _moba_prompt(proj_p, cos_p, sin_p, bp, n_heads, d_main, gate_off)
            mix_s, k_s = _moba_decode(proj_s, kc, vc, page_table, j, cos_s, sin_s, bs, n_heads, d_main, gate_off, ts)
            outs["moba_k_p"].append(k_p.reshape(bp, tp, n_heads, HEAD_DIM))
            outs["moba_v_p"].append(v_p.reshape(bp, tp, n_heads, HEAD_DIM))
            outs["moba_k_s"].append(k_s.reshape(bs, rows, n_heads, HEAD_DIM)[:, :ts])
            outs["moba_v_s"].append(proj_s[:, 2 * d_main:3 * d_main].reshape(bs, rows, n_heads, HEAD_DIM)[:, :ts])
        kv = _project(_rmsnorm(mem, mem_gains, l, BF16, NORM_ROWS), w_mem_kv, l, MEM_PROJ_ROWS, PROJ_COLS)
        outs["memk_p"].append(kv[:, :d_cross].reshape(bp, mem_len, N_MEM_HEADS, mem_hd))
        outs["memv_p"].append(kv[:, d_cross:].reshape(bp, mem_len, N_MEM_HEADS, mem_hd))
        cross_p = _cross_attention(
            proj_p, kv, kv,
            pl.BlockSpec((mem_len, mem_hd), lambda b, h, i: (b, h)),
            pl.BlockSpec((mem_len, mem_hd), lambda b, h, i: (b, N_MEM_HEADS + h)),
            bp, XATTN_ROWS, q_off, gate_off + d_main, d_cross)
        cached = pl.BlockSpec((None, None, mem_len, mem_hd), functools.partial(lambda b, h, i, l: (l, b, 0, h), l=l))
        cross_s = _cross_attention(proj_s, mem_kc, mem_vc, cached, cached, bs, rows, q_off, gate_off + d_main, d_cross)
        xp, xs = _out_project(mix_p, cross_p, w_out, l, xp, OUT_PROJ_ROWS, OUT_PROJ_COLS, extra=(mix_s, cross_s, xs))

    final_g = final_norm_g.reshape(1, 1, d_model)
    y_prompt = _rmsnorm(xp, final_g, 0, F32, NORM_ROWS).reshape(bp, tp, d_model)
    y_sample = _rmsnorm(xs, final_g, 0, F32, bs * rows).reshape(bs, rows, d_model)[:, :ts]
    return (y_prompt, y_sample,
            jnp.stack(outs["moba_k_p"]), jnp.stack(outs["moba_v_p"]), jnp.stack(outs["hgrn_p"]),
            jnp.stack(outs["memk_p"]), jnp.stack(outs["memv_p"]),
            jnp.stack(outs["moba_k_s"]), jnp.stack(outs["moba_v_s"]), jnp.stack(outs["hgrn_s"]))
```

```python
import functools

import jax
import jax.numpy as jnp
from jax import lax
from jax.experimental import pallas as pl
from jax.experimental.pallas import tpu as pltpu

F32 = jnp.float32
BF16 = jnp.bfloat16

HEAD_DIM = 128
N_MEM_HEADS = 4
PAGE_SIZE = 128
MOBA_BLOCK = 256
MOBA_TOPK = 3
N_MIXERS = 2
ROPE_THETA = 10000.0
RMS_EPS = 1e-6
MIN_FORGET = 1e-30
LOG2_E = 1.4426950408889634

LANES = 128
SUBLANES = 8
VMEM_LIMIT_BYTES = 52 * 1024 * 1024

PROJ_ROWS = 512
OUT_PROJ_ROWS = 512
MEM_PROJ_ROWS = 512
PROJ_COLS = 1024
OUT_PROJ_COLS = 1024
NORM_ROWS = 256
XATTN_ROWS = 1024
GLA_CHUNK = 128
SAMPLE_ROWS = SUBLANES


def _params(*semantics):
    return pltpu.CompilerParams(dimension_semantics=semantics, vmem_limit_bytes=VMEM_LIMIT_BYTES)


def _nt_dot(a, b):
    return lax.dot_general(a.astype(BF16), b.astype(BF16), (((1,), (1,)), ((), ())),
                           preferred_element_type=F32)


def _nn_dot(a, b):
    return jnp.dot(a.astype(BF16), b.astype(BF16), preferred_element_type=F32)


def _silu(x):
    return x * jax.nn.sigmoid(x)


def _rmsnorm_body(x_ref, g_ref, o_ref):
    x = x_ref[...]
    inv = lax.rsqrt(jnp.mean(x * x, axis=-1, keepdims=True) + RMS_EPS)
    o_ref[...] = (x * inv * g_ref[...]).astype(o_ref.dtype)


def _rmsnorm(x, gains, layer, out_dtype, tm):
    m, d = x.shape
    return pl.pallas_call(
        _rmsnorm_body,
        grid=(m // tm,),
        in_specs=[pl.BlockSpec((tm, d), lambda i: (i, 0)),
                  pl.BlockSpec((None, 1, d), lambda i: (layer, 0, 0))],
        out_specs=pl.BlockSpec((tm, d), lambda i: (i, 0)),
        out_shape=jax.ShapeDtypeStruct((m, d), out_dtype),
        compiler_params=_params("arbitrary"),
        name="rmsnorm",
    )(x, gains)


def _stage_weights(w_hbm, stage_ref, wb_ref, sem, layer, tn, on_new_block=None):
    j, i = pl.program_id(0), pl.program_id(1)
    n_slots = stage_ref.shape[0]

    def fetch(jj):
        cols = pl.ds(pl.multiple_of(jj * tn, tn), tn)
        slot = jj % n_slots
        return pltpu.make_async_copy(w_hbm.at[layer, :, cols], stage_ref.at[slot], sem.at[slot])

    def prefetch_next():
        @pl.when(j + 1 < pl.num_programs(0))
        def _():
            fetch(j + 1).start()

    @pl.when(i == 0)
    def _():
        @pl.when(j == 0)
        def _():
            fetch(0).start()

        if n_slots > 1:
            prefetch_next()
        fetch(j).wait()
        wb_ref[...] = stage_ref[j % n_slots].astype(BF16)
        if n_slots == 1:
            prefetch_next()
        if on_new_block is not None:
            on_new_block()


def _weight_scratch(kdim, tn, n_row_tiles):
    n_slots = 2 if n_row_tiles == 1 else 1
    return [pltpu.VMEM((n_slots, kdim, tn), F32), pltpu.VMEM((kdim, tn), BF16), pltpu.SemaphoreType.DMA((n_slots,))]


def _proj_body(*refs, layer, tn, with_extra):
    if with_extra:
        x_ref, xe_ref, w_hbm, o_ref, oe_ref, stage_ref, wb_ref, sem = refs
    else:
        x_ref, w_hbm, o_ref, stage_ref, wb_ref, sem = refs

    def extra_rows():
        oe_ref[...] = jnp.dot(xe_ref[...], wb_ref[...], preferred_element_type=F32)

    _stage_weights(w_hbm, stage_ref, wb_ref, sem, layer, tn, extra_rows if with_extra else None)
    o_ref[...] = jnp.dot(x_ref[...], wb_ref[...], preferred_element_type=F32)


def _project(x, w, layer, tm, tn, extra=None):
    m, kdim = x.shape
    n = w.shape[-1]
    in_specs = [pl.BlockSpec((tm, kdim), lambda j, i: (i, 0))]
    out_specs = [pl.BlockSpec((tm, tn), lambda j, i: (i, j))]
    out_shape = [jax.ShapeDtypeStruct((m, n), F32)]
    args = [x]
    if extra is not None:
        me = extra.shape[0]
        in_specs.append(pl.BlockSpec((me, kdim), lambda j, i: (0, 0)))
        out_specs.append(pl.BlockSpec((me, tn), lambda j, i: (0, j)))
        out_shape.append(jax.ShapeDtypeStruct((me, n), F32))
        args.append(extra)
    outs = pl.pallas_call(
        functools.partial(_proj_body, layer=layer, tn=tn, with_extra=extra is not None),
        grid=(n // tn, m // tm),
        in_specs=in_specs + [pl.BlockSpec(memory_space=pl.ANY)],
        out_specs=out_specs,
        out_shape=out_shape,
        scratch_shapes=_weight_scratch(kdim, tn, m // tm),
        compiler_params=_params("arbitrary", "arbitrary"),
        name="in_proj",
    )(*args, w)
    return outs if extra is not None else outs[0]


def _outproj_body(*refs, d_main, layer, tn, with_extra):
    if with_extra:
        a_ref, c_ref, r_ref, ae_ref, ce_ref, re_ref, w_hbm, o_ref, oe_ref, stage_ref, wb_ref, sem = refs
    else:
        a_ref, c_ref, r_ref, w_hbm, o_ref, stage_ref, wb_ref, sem = refs

    def rows(a, c, r):
        acc = jnp.dot(a[...], wb_ref[:d_main, :], preferred_element_type=F32)
        acc = acc + jnp.dot(c[...], wb_ref[d_main:, :], preferred_element_type=F32)
        return r[...] + acc

    def extra_rows():
        oe_ref[...] = rows(ae_ref, ce_ref, re_ref)

    _stage_weights(w_hbm, stage_ref, wb_ref, sem, layer, tn, extra_rows if with_extra else None)
    o_ref[...] = rows(a_ref, c_ref, r_ref)


def _out_project(mix_main, mix_cross, w, layer, resid, tm, tn, extra=None):
    m, d_main = mix_main.shape
    d_cross = mix_cross.shape[1]
    kdim, n = w.shape[1], w.shape[2]

    def specs(rows_per_step, row_index):
        return [pl.BlockSpec((rows_per_step, d_main), lambda j, i: (row_index(i), 0)),
                pl.BlockSpec((rows_per_step, d_cross), lambda j, i: (row_index(i), 0)),
                pl.BlockSpec((rows_per_step, tn), lambda j, i: (row_index(i), j))]

    in_specs = specs(tm, lambda i: i)
    out_specs = [pl.BlockSpec((tm, tn), lambda j, i: (i, j))]
    out_shape = [jax.ShapeDtypeStruct((m, n), F32)]
    args = [mix_main, mix_cross, resid]
    if extra is not None:
        me = extra[0].shape[0]
        in_specs += specs(me, lambda i: 0)
        out_specs.append(pl.BlockSpec((me, tn), lambda j, i: (0, j)))
        out_shape.append(jax.ShapeDtypeStruct((me, n), F32))
        args += list(extra)
    outs = pl.pallas_call(
        functools.partial(_outproj_body, d_main=d_main, layer=layer, tn=tn, with_extra=extra is not None),
        grid=(n // tn, m // tm),
        in_specs=in_specs + [pl.BlockSpec(memory_space=pl.ANY)],
        out_specs=out_specs,
        out_shape=out_shape,
        scratch_shapes=_weight_scratch(kdim, tn, m // tm),
        compiler_params=_params("arbitrary", "arbitrary"),
        name="out_proj",
    )(*args, w)
    return outs if extra is not None else outs[0]


def _hgrn_body(q_ref, f_ref, i_ref, g_ref, lbl_ref, og_ref, *rest, layer_j, n_valid, has_s0, heads):
    rest = list(rest)
    s0_ref = rest.pop(0) if has_s0 else None
    o_ref, sfin_ref, mask_s = rest
    for h in range(heads):
        lanes = pl.ds(h * HEAD_DIM, HEAD_DIM)
        _hgrn_head(q_ref.at[:, lanes], f_ref.at[:, lanes], i_ref.at[:, lanes], g_ref.at[:, lanes],
                   lbl_ref.at[:, lanes], og_ref, s0_ref.at[h] if has_s0 else None,
                   o_ref.at[:, lanes], sfin_ref.at[h], mask_s, layer_j=layer_j, n_valid=n_valid, first=h == 0)


def _hgrn_head(q_ref, f_ref, i_ref, g_ref, lbl_ref, og_ref, s0_ref, o_ref, sfin_ref, mask_s, *,
               layer_j, n_valid, first):
    has_s0 = s0_ref is not None
    c = GLA_CHUNK
    n_rows = q_ref.shape[0]

    n_layers = lbl_ref.shape[0]
    logits = [lbl_ref[i:i + 1, :] for i in range(n_layers)]
    mx = functools.reduce(jnp.maximum, logits)
    es = [jnp.exp(l - mx) for l in logits]
    tot = functools.reduce(lambda a, b: a + b, es)
    ws = [e / tot for e in es]
    csum = ws[0]
    for i in range(1, layer_j + 1):
        csum = csum + ws[i]
    lb = csum - ws[0]
    og = og_ref[...]

    row = lax.broadcasted_iota(jnp.int32, (c, HEAD_DIM), 0)
    tt = lax.broadcasted_iota(jnp.int32, (c, c), 0)
    ss = lax.broadcasted_iota(jnp.int32, (c, c), 1)
    n_levels = (min(n_valid, c) - 1).bit_length()
    if first:
        for lvl in range(n_levels):
            m = 1 << lvl
            pair = ((tt >> (lvl + 1)) == (ss >> (lvl + 1))) & ((tt & m) != 0) & ((ss & m) == 0)
            mask_s[lvl] = jnp.where(pair, 1.0, 0.0)
        mask_s[n_levels] = jnp.where(tt == ss, 1.0, 0.0)

    def prefix_sum(x):
        s = 1
        while s < c:
            x = x + jnp.where(row >= s, pltpu.roll(x, s, 0), 0.0)
            s *= 2
        return x

    def gates_part(fxc):
        f = lb + (1.0 - lb) * jax.nn.sigmoid(fxc)
        logf = jnp.log(jnp.maximum(f, MIN_FORGET))
        kk = 1.0 - f
        if n_valid < c:
            ok = row < n_valid
            logf = jnp.where(ok, logf, 0.0)
            kk = jnp.where(ok, kk, 0.0)
        return kk, prefix_sum(logf)

    def local_part(qc, kk, cum, vc):
        att = _nt_dot(qc, kk) * mask_s[n_levels]
        end = cum
        for lvl in range(n_levels):
            m = 1 << lvl
            if 2 * m >= SUBLANES:
                g = c // (2 * m)
                ref = jnp.broadcast_to(cum.reshape(g, 2 * m, HEAD_DIM)[:, m - 1:m, :],
                                       (g, 2 * m, HEAD_DIM)).reshape(c, HEAD_DIM)
            else:
                upper = (row & m) != 0
                ref = jnp.where(upper, pltpu.roll(end, m, 0), end)
                end = jnp.where(upper, end, pltpu.roll(end, c - m, 0))
            e = jnp.exp2(jnp.abs(cum - ref) * (-LOG2_E))
            att = att + _nt_dot(qc * e, kk * e) * mask_s[lvl]
        last = cum[c - 1:c, :]
        intra = _nn_dot(att, vc)
        q_dec = (qc * jnp.exp(cum)).astype(BF16)
        kv = _nn_dot(vc.T, kk * jnp.exp(last - cum))
        return intra, q_dec, kv, jnp.exp(last)

    def state_part(intra, q_dec, kv, dec, st, gate):
        o = _nt_dot(q_dec, st) + intra
        y = o * lax.rsqrt(jnp.mean(o * o, axis=-1, keepdims=True) + RMS_EPS) * og
        return y * _silu(gate), st * dec + kv

    st0 = s0_ref[...].T if has_s0 else jnp.zeros((HEAD_DIM, HEAD_DIM), F32)
    if n_rows < c:
        pad = jnp.zeros((c - n_rows, HEAD_DIM), F32)
        grow = lambda ref: jnp.concatenate([ref[...], pad], axis=0)
        y, st = state_part(*local_part(grow(q_ref), *gates_part(grow(f_ref)), grow(i_ref)), st0, grow(g_ref))
        o_ref[...] = y[:n_rows].astype(o_ref.dtype)
    else:
        def body(ci, st):
            rows = pl.ds(pl.multiple_of(ci * c, c), c)
            local = local_part(q_ref[rows, :], *gates_part(f_ref[rows, :]), i_ref[rows, :])
            y, st = state_part(*local, st, g_ref[rows, :])
            o_ref[rows, :] = y.astype(o_ref.dtype)
            return st

        st = lax.fori_loop(0, n_rows // c, body, st0, unroll=True)
    sfin_ref[...] = st.T


def _hgrn(proj, gate, gate_col, lb_logits, out_g, layer_j, n_seq, n_heads, d_main, s0, n_valid):
    t = proj.shape[0] // n_seq
    heads = n_heads if t < GLA_CHUNK else 1
    width = heads * HEAD_DIM
    groups = n_heads // heads
    assert gate_col % width == 0
    col = lambda first_block: (lambda b, h: (b, first_block + h))
    in_specs = [pl.BlockSpec((t, width), col(0)),
                pl.BlockSpec((t, width), col(groups)),
                pl.BlockSpec((t, width), col(2 * groups)),
                pl.BlockSpec((t, width), col(gate_col // width)),
                pl.BlockSpec((lb_logits.shape[0], width), lambda b, h: (0, h)),
                pl.BlockSpec((None, 1, HEAD_DIM), lambda b, h: (layer_j, 0, 0))]
    args = [proj, proj, proj, gate, lb_logits, out_g]
    if s0 is not None:
        in_specs.append(pl.BlockSpec((None, None, heads, HEAD_DIM, HEAD_DIM),
                                     lambda b, h: (layer_j, b, h, 0, 0)))
        args.append(s0)
    return pl.pallas_call(
        functools.partial(_hgrn_body, layer_j=layer_j, n_valid=n_valid, has_s0=s0 is not None, heads=heads),
        grid=(n_seq, groups),
        in_specs=in_specs,
        out_specs=[pl.BlockSpec((t, width), lambda b, h: (b, h)),
                   pl.BlockSpec((None, heads, HEAD_DIM, HEAD_DIM), lambda b, h: (b, h, 0, 0))],
        out_shape=[jax.ShapeDtypeStruct((n_seq * t, d_main), BF16),
                   jax.ShapeDtypeStruct((n_seq, n_heads, HEAD_DIM, HEAD_DIM), F32)],
        scratch_shapes=[pltpu.VMEM((GLA_CHUNK.bit_length(), GLA_CHUNK, GLA_CHUNK), F32)],
        compiler_params=_params("arbitrary", "arbitrary"),
        name="hgrn2",
    )(*args)


def _rope_tables(pos):
    half = HEAD_DIM // 2
    inv = ROPE_THETA ** (-jnp.arange(half, dtype=F32) / half)
    ang = pos.astype(F32)[:, None] * inv[None, :]
    cos, sin = jnp.cos(ang), jnp.sin(ang)
    return jnp.concatenate([cos, cos], axis=-1), jnp.concatenate([-sin, sin], axis=-1)


def _rope(x, cos, sin):
    return x * cos + pltpu.roll(x, HEAD_DIM // 2, 1) * sin


def _moba_body(q_ref, k_ref, v_ref, g_ref, cos_ref, sin_ref, kout_ref, vout_ref, o_ref,
               qf_s, qb_s, kb_s, vt_s, km_s, s_s, p_s):
    blk = MOBA_BLOCK
    nblk = q_ref.shape[0] // blk
    scale = HEAD_DIM ** -0.5

    t = q_ref.shape[0]
    head_slot = pl.program_id(1) % SUBLANES
    kout_rows = kout_ref.reshape(t * SUBLANES, HEAD_DIM)
    vout_rows = vout_ref.reshape(t * SUBLANES, HEAD_DIM)

    km_s[...] = jnp.zeros_like(km_s)
    for j in range(nblk):
        rows = pl.ds(j * blk, blk)
        head_rows = pl.ds(j * blk * SUBLANES + head_slot, blk, stride=SUBLANES)
        cos, sin = cos_ref[rows, :], sin_ref[rows, :]
        q = _rope(q_ref[rows, :], cos, sin)
        k = _rope(k_ref[rows, :], cos, sin)
        v = v_ref[rows, :]
        kout_rows[head_rows, :] = k
        vout_rows[head_rows, :] = v
        qf_s[j] = q
        qb_s[j] = q.astype(BF16)
        kb_s[rows, :] = k.astype(BF16)
        vt_s[:HEAD_DIM, rows] = v.T.astype(BF16)
        vt_s[HEAD_DIM:, rows] = jnp.ones((vt_s.shape[0] - HEAD_DIM, blk), BF16)
        km_s[j:j + 1, :] = jnp.sum(k, axis=0, keepdims=True) * (1.0 / blk)

    npad = km_s.shape[0]
    jidx = lax.broadcasted_iota(jnp.int32, (npad, blk), 0)
    causal = (lax.broadcasted_iota(jnp.int32, (blk, blk), 0) <=
              lax.broadcasted_iota(jnp.int32, (blk, blk), 1))

    first = lambda i: (i * (i + 1) // 2) * blk
    col_max = []
    for i in range(nblk):
        gate = lax.dot_general(km_s[...], qf_s[i], (((1,), (1,)), ((), ())),
                               precision=lax.Precision.HIGHEST, preferred_element_type=F32)
        cand = jidx < i
        gm = jnp.where(cand, gate, -jnp.inf)
        rank = jnp.zeros((npad, blk), jnp.int32)
        for jp in range(i):
            gj = gm[jp:jp + 1, :]
            rank = rank + jnp.where((gj > gm) | ((gj == gm) & (jp < jidx)), 1, 0)
        sel = jnp.where(cand & (rank < MOBA_TOPK) & (jnp.abs(gate) < jnp.inf), 1.0, 0.0)

        m = jnp.full((1, blk), -jnp.inf, F32)
        qi = qb_s[i]
        for j in range(i + 1):
            mask = causal if j == i else jnp.broadcast_to(sel[j:j + 1, :], (blk, blk)) > 0.0
            st = jnp.where(mask, _nt_dot(kb_s[pl.ds(j * blk, blk), :], qi), -jnp.inf)
            s_s[pl.ds(first(i) + j * blk, blk), :] = st
            m = jnp.maximum(m, jnp.max(st, axis=0, keepdims=True))
        col_max.append(m)
    for i in range(nblk):
        for j in range(i + 1):
            rows = pl.ds(first(i) + j * blk, blk)
            p_s[rows, :] = jnp.exp2((s_s[rows, :] - col_max[i]) * (scale * LOG2_E)).astype(BF16)
    for i in range(nblk):
        n = (i + 1) * blk
        acc = jnp.dot(vt_s[:, :n], p_s[pl.ds(first(i), n), :], preferred_element_type=F32)
        out = (acc[:HEAD_DIM] / acc[HEAD_DIM:HEAD_DIM + 1]).T
        rows = pl.ds(i * blk, blk)
        o_ref[rows, :] = (out * _silu(g_ref[rows, :])).astype(o_ref.dtype)


def _moba_prompt(proj, cos, sin, n_seq, n_heads, d_main, gate_off):
    t = proj.shape[0] // n_seq
    assert t % MOBA_BLOCK == 0
    nblk = t // MOBA_BLOCK
    hb = d_main // HEAD_DIM
    col = lambda off: (lambda b, h: (b, off + h))
    tab = pl.BlockSpec((t, HEAD_DIM), lambda b, h: (0, 0))
    assert n_heads % SUBLANES == 0
    cache_rows = pl.BlockSpec((None, t, None, SUBLANES, HEAD_DIM), lambda b, h: (b, 0, h // SUBLANES, 0, 0))
    cache_shape = jax.ShapeDtypeStruct((n_seq, t, n_heads // SUBLANES, SUBLANES, HEAD_DIM), F32)
    return pl.pallas_call(
        _moba_body,
        grid=(n_seq, n_heads),
        in_specs=[pl.BlockSpec((t, HEAD_DIM), col(0)),
                  pl.BlockSpec((t, HEAD_DIM), col(hb)),
                  pl.BlockSpec((t, HEAD_DIM), col(2 * hb)),
                  pl.BlockSpec((t, HEAD_DIM), col(gate_off // HEAD_DIM)),
                  tab, tab],
        out_specs=[cache_rows, cache_rows, pl.BlockSpec((t, HEAD_DIM), lambda b, h: (b, h))],
        out_shape=[cache_shape, cache_shape, jax.ShapeDtypeStruct((n_seq * t, d_main), BF16)],
        scratch_shapes=[pltpu.VMEM((nblk, MOBA_BLOCK, HEAD_DIM), F32),
                        pltpu.VMEM((nblk, MOBA_BLOCK, HEAD_DIM), BF16),
                        pltpu.VMEM((t, HEAD_DIM), BF16),
                        pltpu.VMEM((HEAD_DIM + 2 * SUBLANES, t), BF16),
                        pltpu.VMEM((-(-nblk // SUBLANES) * SUBLANES, HEAD_DIM), F32),
                        pltpu.VMEM((nblk * (nblk + 1) // 2 * MOBA_BLOCK, MOBA_BLOCK), F32),
                        pltpu.VMEM((nblk * (nblk + 1) // 2 * MOBA_BLOCK, MOBA_BLOCK), BF16)],
        compiler_params=_params("arbitrary", "arbitrary"),
        name="moba_prompt",
    )(proj, proj, proj, proj, cos, sin)


def _moba_parts_body(pt_ref, k0_ref, k1_ref, v0_ref, v1_ref, q_ref, kn_ref, vn_ref, cos_ref, sin_ref,
                     po_ref, pm_ref, pl_ref, pg_ref, knew_ref, s_s, qk_s, *, n_heads, n_new, n_past_blocks):
    del pt_ref
    j = pl.program_id(1)
    scale = HEAD_DIM ** -0.5
    cos, sin = cos_ref[...], sin_ref[...]
    rows = q_ref.shape[0]

    def put(ref, h, val):
        ref[:, h * HEAD_DIM:(h + 1) * HEAD_DIM] = jnp.broadcast_to(val, (rows, HEAD_DIM))

    @pl.when(j < n_past_blocks)
    def _():
        for h in range(n_heads):
            hs = slice(h * HEAD_DIM, (h + 1) * HEAD_DIM)
            hr = slice(h * rows, (h + 1) * rows)
            head_rows = pl.ds(h, PAGE_SIZE, stride=n_heads)
            kh = jnp.concatenate([k0_ref[head_rows, :], k1_ref[head_rows, :]], axis=0)
            qh = _rope(q_ref[:, hs], cos, sin)
            s_s[hr, :] = _nt_dot(qh, kh) * scale
            qk_s[hr, :] = qh * (jnp.sum(kh, axis=0, keepdims=True) * (1.0 / MOBA_BLOCK))
        s = s_s[...]
        m = jnp.max(s, axis=1, keepdims=True)
        p = jnp.exp(s - m)
        s_s[...] = p
        lsum = jnp.sum(p, axis=1, keepdims=True)
        gate = jnp.sum(qk_s[...], axis=1, keepdims=True)
        for h in range(n_heads):
            hs = slice(h * HEAD_DIM, (h + 1) * HEAD_DIM)
            hr = slice(h * rows, (h + 1) * rows)
            head_rows = pl.ds(h, PAGE_SIZE, stride=n_heads)
            vh = jnp.concatenate([v0_ref[head_rows, :], v1_ref[head_rows, :]], axis=0)
            po_ref[:, hs] = _nn_dot(s_s[hr, :], vh)
            put(pm_ref, h, m[hr])
            put(pl_ref, h, lsum[hr])
            put(pg_ref, h, gate[hr])

    @pl.when(j == n_past_blocks)
    def _():
        t_i = lax.broadcasted_iota(jnp.int32, (rows, 1), 0)
        for h in range(n_heads):
            hs = slice(h * HEAD_DIM, (h + 1) * HEAD_DIM)
            qh = _rope(q_ref[:, hs], cos, sin)
            kh = _rope(kn_ref[:, hs], cos, sin)
            vh = vn_ref[:, hs]
            knew_ref[:, hs] = kh
            logit = [jnp.where(t_i >= s, jnp.sum(qh * kh[s:s + 1, :], axis=1, keepdims=True) * scale, -jnp.inf)
                     for s in range(n_new)]
            m = functools.reduce(jnp.maximum, logit)
            ps = [jnp.exp(lg - m) for lg in logit]
            po_ref[:, hs] = functools.reduce(lambda a, b: a + b, [p * vh[s:s + 1, :] for s, p in enumerate(ps)])
            put(pm_ref, h, m)
            put(pl_ref, h, functools.reduce(lambda a, b: a + b, ps))
            put(pg_ref, h, jnp.zeros((rows, 1), F32))


def _moba_combine_body(po_ref, pm_ref, pl_ref, pg_ref, g_ref, o_ref, cur_s, sel_s, *, n_past_blocks):
    nb = n_past_blocks
    shape = cur_s.shape
    jidx = lax.broadcasted_iota(jnp.int32, shape, 0)
    cur_s[...] = pg_ref[:nb]
    sel_s[...] = jnp.zeros(shape, F32)
    for _ in range(MOBA_TOPK):
        cur = cur_s[...]
        best = jnp.max(cur, axis=0, keepdims=True)
        first = jnp.min(jnp.where(cur == best, jidx, nb), axis=0, keepdims=True)
        pick = jidx == first
        sel_s[...] = jnp.where(pick & (jnp.abs(best) < jnp.inf), 1.0, sel_s[...])
        cur_s[...] = jnp.where(pick, -jnp.inf, cur)
    sel = sel_s[...] > 0.0
    m_own = pm_ref[nb]
    m_all = jnp.maximum(m_own, jnp.max(jnp.where(sel, pm_ref[:nb], -jnp.inf), axis=0))
    w = jnp.where(sel, jnp.exp(pm_ref[:nb] - m_all[None]), 0.0)
    w_own = jnp.exp(m_own - m_all)
    den = jnp.sum(w * pl_ref[:nb], axis=0) + w_own * pl_ref[nb]
    num = jnp.sum(w * po_ref[:nb], axis=0) + w_own * po_ref[nb]
    o_ref[...] = ((num / den) * _silu(g_ref[...])).astype(o_ref.dtype)


def _moba_decode(proj, k_cache, v_cache, page_table, layer_j, cos, sin, n_seq, n_heads, d_main, gate_off, n_new):
    rows = SAMPLE_ROWS
    n_pages = page_table.shape[1]
    pages_per_block = MOBA_BLOCK // PAGE_SIZE
    assert pages_per_block == 2 and n_pages % pages_per_block == 0
    nb = n_pages // pages_per_block
    page_rows = PAGE_SIZE * n_heads

    def page(p):
        def index(b, j, pt):
            return (layer_j, pt[b, jnp.minimum(pages_per_block * j + p, n_pages - 1)], 0, 0)
        return pl.BlockSpec((None, None, page_rows, HEAD_DIM), index)

    tok = lambda off: pl.BlockSpec((rows, d_main), lambda b, j, pt: (b, off))
    tab = pl.BlockSpec((rows, HEAD_DIM), lambda b, j, pt: (0, 0))
    part = pl.BlockSpec((None, None, rows, d_main), lambda b, j, pt: (b, j, 0, 0))
    part_shape = jax.ShapeDtypeStruct((n_seq, nb + 1, rows, d_main), F32)
    po, pm, pls, pg, k_new = pl.pallas_call(
        functools.partial(_moba_parts_body, n_heads=n_heads, n_new=n_new, n_past_blocks=nb),
        grid_spec=pltpu.PrefetchScalarGridSpec(
            num_scalar_prefetch=1,
            grid=(n_seq, nb + 1),
            in_specs=[page(0), page(1), page(0), page(1), tok(0), tok(1), tok(2), tab, tab],
            out_specs=[part, part, part, part,
                       pl.BlockSpec((rows, d_main), lambda b, j, pt: (b, 0))],
            scratch_shapes=[pltpu.VMEM((n_heads * rows, MOBA_BLOCK), F32),
                            pltpu.VMEM((n_heads * rows, HEAD_DIM), F32)]),
        out_shape=[part_shape, part_shape, part_shape, part_shape,
                   jax.ShapeDtypeStruct((n_seq * rows, d_main), F32)],
        compiler_params=_params("arbitrary", "arbitrary"),
        name="moba_decode_parts",
    )(page_table, k_cache, k_cache, v_cache, v_cache, proj, proj, proj, cos, sin)

    whole = pl.BlockSpec((None, nb + 1, rows, d_main), lambda b: (b, 0, 0, 0))
    gate_blk = 1024
    assert gate_off % gate_blk == 0 and d_main % gate_blk == 0
    n_gate = d_main // gate_blk
    gate_specs = [pl.BlockSpec((rows, gate_blk), functools.partial(lambda b, i: (b, gate_off // gate_blk + i), i=i))
                  for i in range(n_gate)]

    def body(po_ref, pm_ref, pl_ref, pg_ref, *rest):
        g_refs, (o_ref, cur_s, sel_s, g_s) = rest[:n_gate], rest[n_gate:]
        for i, g_ref in enumerate(g_refs):
            g_s[:, i * gate_blk:(i + 1) * gate_blk] = g_ref[...]
        _moba_combine_body(po_ref, pm_ref, pl_ref, pg_ref, g_s, o_ref, cur_s, sel_s, n_past_blocks=nb)

    mix = pl.pallas_call(
        body,
        grid=(n_seq,),
        in_specs=[whole, whole, whole, whole] + gate_specs,
        out_specs=pl.BlockSpec((rows, d_main), lambda b: (b, 0)),
        out_shape=jax.ShapeDtypeStruct((n_seq * rows, d_main), BF16),
        scratch_shapes=[pltpu.VMEM((nb, rows, d_main), F32), pltpu.VMEM((nb, rows, d_main), F32),
                        pltpu.VMEM((rows, d_main), F32)],
        compiler_params=_params("arbitrary"),
        name="moba_decode_combine",
    )(po, pm, pls, pg, *([proj] * n_gate))
    return mix, k_new


def _xattn_body(q_ref, k_ref, v_ref, g_ref, o_ref):
    scale = q_ref.shape[-1] ** -0.5
    s = _nt_dot(q_ref[...], k_ref[...]) * scale
    m = jnp.max(s, axis=-1, keepdims=True)
    p = jnp.exp(s - m)
    p = p / jnp.sum(p, axis=-1, keepdims=True)
    o_ref[...] = (_nn_dot(p, v_ref[...]) * _silu(g_ref[...])).astype(o_ref.dtype)


def _cross_attention(proj, mem_k, mem_v, k_spec, v_spec, n_seq, tq, q_off, gate_off, d_cross):
    t = proj.shape[0] // n_seq
    hd = d_cross // N_MEM_HEADS
    nq = t // tq
    row = lambda off: (lambda b, h, i: (b * nq + i, off + h))
    return pl.pallas_call(
        _xattn_body,
        grid=(n_seq, N_MEM_HEADS, nq),
        in_specs=[pl.BlockSpec((tq, hd), row(q_off // hd)),
                  k_spec, v_spec,
                  pl.BlockSpec((tq, hd), row(gate_off // hd))],
        out_specs=pl.BlockSpec((tq, hd), row(0)),
        out_shape=jax.ShapeDtypeStruct((n_seq * t, d_cross), BF16),
        compiler_params=_params("arbitrary", "arbitrary", "arbitrary"),
        name="mem_xattn",
    )(proj, mem_k, mem_v, proj)


def kernel(x_prompt, x_sample, cache_moba_k, cache_moba_v, state_hgrn, cache_mem_k, cache_mem_v,
           page_table, mem_prompt, norm_g, w_in, w_out, hgrn_lb_logits, hgrn_out_g, mem_norm_g,
           w_mem_kv, final_norm_g):
    bp, tp, d_model = x_prompt.shape
    bs, ts, _ = x_sample.shape
    depth = w_in.shape[0]
    d_mix = w_out.shape[1]
    d_cross = w_mem_kv.shape[2] // 2
    d_main = d_mix - d_cross
    n_heads = d_main // HEAD_DIM
    mem_len = mem_prompt.shape[1]
    mem_hd = d_cross // N_MEM_HEADS
    n_pages = page_table.shape[1]
    past_len = n_pages * PAGE_SIZE
    q_off = 3 * d_main
    gate_off = 3 * d_main + d_cross
    rows = SAMPLE_ROWS
    assert ts <= rows

    gains = norm_g.reshape(depth, 1, d_model)
    mem_gains = mem_norm_g.reshape(depth, 1, d_model)
    out_gains = hgrn_out_g.reshape(-1, 1, HEAD_DIM)

    xp = x_prompt.reshape(bp * tp, d_model)
    xs = jnp.pad(x_sample, ((0, 0), (0, rows - ts), (0, 0))).reshape(bs * rows, d_model)
    mem = mem_prompt.reshape(bp * mem_len, d_model)

    cos_p, sin_p = _rope_tables(jnp.arange(tp, dtype=jnp.int32))
    cos_s, sin_s = _rope_tables(past_len + jnp.arange(rows, dtype=jnp.int32))
    kc = cache_moba_k.reshape(cache_moba_k.shape[0], cache_moba_k.shape[1], PAGE_SIZE * n_heads, HEAD_DIM)
    vc = cache_moba_v.reshape(kc.shape)
    mem_kc = cache_mem_k.reshape(depth, bs, mem_len, d_cross)
    mem_vc = cache_mem_v.reshape(depth, bs, mem_len, d_cross)

    tm = PROJ_ROWS
    outs = {k: [] for k in ("moba_k_p", "moba_v_p", "hgrn_p", "memk_p", "memv_p", "moba_k_s", "moba_v_s", "hgrn_s")}
    for l in range(depth):
        j = l // N_MIXERS
        proj_p, proj_s = _project(_rmsnorm(xp, gains, l, BF16, NORM_ROWS), w_in, l, tm, PROJ_COLS,
                                  extra=_rmsnorm(xs, gains, l, BF16, bs * rows))
        if l % N_MIXERS == 0:
            mix_p, st_p = _hgrn(proj_p, proj_p, gate_off, hgrn_lb_logits, out_gains, j, bp, n_heads, d_main,
                                None, GLA_CHUNK)
            mix_s, st_s = _hgrn(proj_s, proj_s[:, gate_off:gate_off + d_main], 0, hgrn_lb_logits, out_gains, j, bs,
                                n_heads, d_main, state_hgrn, ts)
            outs["hgrn_p"].append(st_p)
            outs["hgrn_s"].append(st_s)
        else:
            k_p, v_p, mix_p = _moba_prompt(proj_p, cos_p, sin_p, bp, n_heads, d_main, gate_off)
            mix_s, k_s = _moba_decode(proj_s, kc, vc, page_table, j, cos_s, sin_s, bs, n_heads, d_main, gate_off, ts)
            outs["moba_k_p"].append(k_p.reshape(bp, tp, n_heads, HEAD_DIM))
            outs["moba_v_p"].append(v_p.reshape(bp, tp, n_heads, HEAD_DIM))
            outs["moba_k_s"].append(k_s.reshape(bs, rows, n_heads, HEAD_DIM)[:, :ts])
            outs["moba_v_s"].append(proj_s[:, 2 * d_main:3 * d_main].reshape(bs, rows, n_heads, HEAD_DIM)[:, :ts])
        kv = _project(_rmsnorm(mem, mem_gains, l, BF16, NORM_ROWS), w_mem_kv, l, MEM_PROJ_ROWS, PROJ_COLS)
        outs["memk_p"].append(kv[:, :d_cross].reshape(bp, mem_len, N_MEM_HEADS, mem_hd))
        outs["memv_p"].append(kv[:, d_cross:].reshape(bp, mem_len, N_MEM_HEADS, mem_hd))
        cross_p = _cross_attention(
            proj_p, kv, kv,
            pl.BlockSpec((mem_len, mem_hd), lambda b, h, i: (b, h)),
            pl.BlockSpec((mem_len, mem_hd), lambda b, h, i: (b, N_MEM_HEADS + h)),
            bp, XATTN_ROWS, q_off, gate_off + d_main, d_cross)
        cached = pl.BlockSpec((None, None, mem_len, mem_hd), functools.partial(lambda b, h, i, l: (l, b, 0, h), l=l))
        cross_s = _cross_attention(proj_s, mem_kc, mem_vc, cached, cached, bs, rows, q_off, gate_off + d_main, d_cross)
        xp, xs = _out_project(mix_p, cross_p, w_out, l, xp, OUT_PROJ_ROWS, OUT_PROJ_COLS, extra=(mix_s, cross_s, xs))

    final_g = final_norm_g.reshape(1, 1, d_model)
    y_prompt = _rmsnorm(xp, final_g, 0, F32, NORM_ROWS).reshape(bp, tp, d_model)
    y_sample = _rmsnorm(xs, final_g, 0, F32, bs * rows).reshape(bs, rows, d_model)[:, :ts]
    return (y_prompt, y_sample,
            jnp.stack(outs["moba_k_p"]), jnp.stack(outs["moba_v_p"]), jnp.stack(outs["hgrn_p"]),
            jnp.stack(outs["memk_p"]), jnp.stack(outs["memv_p"]),
            jnp.stack(outs["moba_k_s"]), jnp.stack(outs["moba_v_s"]), jnp.stack(outs["hgrn_s"]))
```

```python
import functools

import jax
import jax.numpy as jnp
from jax import lax
from jax.experimental import pallas as pl
from jax.experimental.pallas import tpu as pltpu

F32 = jnp.float32
BF16 = jnp.bfloat16

HEAD_DIM = 128
N_MEM_HEADS = 4
PAGE_SIZE = 128
MOBA_BLOCK = 256
MOBA_TOPK = 3
N_MIXERS = 2
ROPE_THETA = 10000.0
RMS_EPS = 1e-6
MIN_FORGET = 1e-30
LOG2_E = 1.4426950408889634

LANES = 128
SUBLANES = 8
VMEM_LIMIT_BYTES = 52 * 1024 * 1024

PROJ_ROWS = 512
OUT_PROJ_ROWS = 512
MEM_PROJ_ROWS = 512
PROJ_COLS = 1024
OUT_PROJ_COLS = 1024
NORM_ROWS = 256
XATTN_ROWS = 1024
GLA_CHUNK = 128
SAMPLE_ROWS = SUBLANES


def _params(*semantics):
    return pltpu.CompilerParams(dimension_semantics=semantics, vmem_limit_bytes=VMEM_LIMIT_BYTES)


def _nt_dot(a, b):
    return lax.dot_general(a.astype(BF16), b.astype(BF16), (((1,), (1,)), ((), ())),
                           preferred_element_type=F32)


def _nn_dot(a, b):
    return jnp.dot(a.astype(BF16), b.astype(BF16), preferred_element_type=F32)


def _silu(x):
    return x * jax.nn.sigmoid(x)


def _rmsnorm_body(x_ref, g_ref, o_ref):
    x = x_ref[...]
    inv = lax.rsqrt(jnp.mean(x * x, axis=-1, keepdims=True) + RMS_EPS)
    o_ref[...] = (x * inv * g_ref[...]).astype(o_ref.dtype)


def _rmsnorm(x, gains, layer, out_dtype, tm):
    m, d = x.shape
    return pl.pallas_call(
        _rmsnorm_body,
        grid=(m // tm,),
        in_specs=[pl.BlockSpec((tm, d), lambda i: (i, 0)),
                  pl.BlockSpec((None, 1, d), lambda i: (layer, 0, 0))],
        out_specs=pl.BlockSpec((tm, d), lambda i: (i, 0)),
        out_shape=jax.ShapeDtypeStruct((m, d), out_dtype),
        compiler_params=_params("arbitrary"),
        name="rmsnorm",
    )(x, gains)


def _stage_weights(w_hbm, stage_ref, wb_ref, sem, layer, tn, on_new_block=None):
    j, i = pl.program_id(0), pl.program_id(1)
    n_slots = stage_ref.shape[0]

    def fetch(jj):
        cols = pl.ds(pl.multiple_of(jj * tn, tn), tn)
        slot = jj % n_slots
        return pltpu.make_async_copy(w_hbm.at[layer, :, cols], stage_ref.at[slot], sem.at[slot])

    def prefetch_next():
        @pl.when(j + 1 < pl.num_programs(0))
        def _():
            fetch(j + 1).start()

    @pl.when(i == 0)
    def _():
        @pl.when(j == 0)
        def _():
            fetch(0).start()

        if n_slots > 1:
            prefetch_next()
        fetch(j).wait()
        wb_ref[...] = stage_ref[j % n_slots].astype(BF16)
        if n_slots == 1:
            prefetch_next()
        if on_new_block is not None:
            on_new_block()


def _weight_scratch(kdim, tn, n_row_tiles):
    n_slots = 2 if n_row_tiles == 1 else 1
    return [pltpu.VMEM((n_slots, kdim, tn), F32), pltpu.VMEM((kdim, tn), BF16), pltpu.SemaphoreType.DMA((n_slots,))]


def _proj_body(*refs, layer, tn, with_extra):
    if with_extra:
        x_ref, xe_ref, w_hbm, o_ref, oe_ref, stage_ref, wb_ref, sem = refs
    else:
        x_ref, w_hbm, o_ref, stage_ref, wb_ref, sem = refs

    def extra_rows():
        oe_ref[...] = jnp.dot(xe_ref[...], wb_ref[...], preferred_element_type=F32)

    _stage_weights(w_hbm, stage_ref, wb_ref, sem, layer, tn, extra_rows if with_extra else None)
    o_ref[...] = jnp.dot(x_ref[...], wb_ref[...], preferred_element_type=F32)


def _project(x, w, layer, tm, tn, extra=None):
    m, kdim = x.shape
    n = w.shape[-1]
    in_specs = [pl.BlockSpec((tm, kdim), lambda j, i: (i, 0))]
    out_specs = [pl.BlockSpec((tm, tn), lambda j, i: (i, j))]
    out_shape = [jax.ShapeDtypeStruct((m, n), F32)]
    args = [x]
    if extra is not None:
        me = extra.shape[0]
        in_specs.append(pl.BlockSpec((me, kdim), lambda j, i: (0, 0)))
        out_specs.append(pl.BlockSpec((me, tn), lambda j, i: (0, j)))
        out_shape.append(jax.ShapeDtypeStruct((me, n), F32))
        args.append(extra)
    outs = pl.pallas_call(
        functools.partial(_proj_body, layer=layer, tn=tn, with_extra=extra is not None),
        grid=(n // tn, m // tm),
        in_specs=in_specs + [pl.BlockSpec(memory_space=pl.ANY)],
        out_specs=out_specs,
        out_shape=out_shape,
        scratch_shapes=_weight_scratch(kdim, tn, m // tm),
        compiler_params=_params("arbitrary", "arbitrary"),
        name="in_proj",
    )(*args, w)
    return outs if extra is not None else outs[0]


def _outproj_body(*refs, d_main, layer, tn, with_extra):
    if with_extra:
        a_ref, c_ref, r_ref, ae_ref, ce_ref, re_ref, w_hbm, o_ref, oe_ref, stage_ref, wb_ref, sem = refs
    else:
        a_ref, c_ref, r_ref, w_hbm, o_ref, stage_ref, wb_ref, sem = refs

    def rows(a, c, r):
        acc = jnp.dot(a[...], wb_ref[:d_main, :], preferred_element_type=F32)
        acc = acc + jnp.dot(c[...], wb_ref[d_main:, :], preferred_element_type=F32)
        return r[...] + acc

    def extra_rows():
        oe_ref[...] = rows(ae_ref, ce_ref, re_ref)

    _stage_weights(w_hbm, stage_ref, wb_ref, sem, layer, tn, extra_rows if with_extra else None)
    o_ref[...] = rows(a_ref, c_ref, r_ref)


def _out_project(mix_main, mix_cross, w, layer, resid, tm, tn, extra=None):
    m, d_main = mix_main.shape
    d_cross = mix_cross.shape[1]
    kdim, n = w.shape[1], w.shape[2]

    def specs(rows_per_step, row_index):
        return [pl.BlockSpec((rows_per_step, d_main), lambda j, i: (row_index(i), 0)),
                pl.BlockSpec((rows_per_step, d_cross), lambda j, i: (row_index(i), 0)),
                pl.BlockSpec((rows_per_step, tn), lambda j, i: (row_index(i), j))]

    in_specs = specs(tm, lambda i: i)
    out_specs = [pl.BlockSpec((tm, tn), lambda j, i: (i, j))]
    out_shape = [jax.ShapeDtypeStruct((m, n), F32)]
    args = [mix_main, mix_cross, resid]
    if extra is not None:
        me = extra[0].shape[0]
        in_specs += specs(me, lambda i: 0)
        out_specs.append(pl.BlockSpec((me, tn), lambda j, i: (0, j)))
        out_shape.append(jax.ShapeDtypeStruct((me, n), F32))
        args += list(extra)
    outs = pl.pallas_call(
        functools.partial(_outproj_body, d_main=d_main, layer=layer, tn=tn, with_extra=extra is not None),
        grid=(n // tn, m // tm),
        in_specs=in_specs + [pl.BlockSpec(memory_space=pl.ANY)],
        out_specs=out_specs,
        out_shape=out_shape,
        scratch_shapes=_weight_scratch(kdim, tn, m // tm),
        compiler_params=_params("arbitrary", "arbitrary"),
        name="out_proj",
    )(*args, w)
    return outs if extra is not None else outs[0]


def _hgrn_body(q_ref, f_ref, i_ref, g_ref, lbl_ref, og_ref, *rest, layer_j, n_valid, has_s0, heads):
    rest = list(rest)
    s0_ref = rest.pop(0) if has_s0 else None
    o_ref, sfin_ref, mask_s = rest
    for h in range(heads):
        lanes = pl.ds(h * HEAD_DIM, HEAD_DIM)
        _hgrn_head(q_ref.at[:, lanes], f_ref.at[:, lanes], i_ref.at[:, lanes], g_ref.at[:, lanes],
                   lbl_ref.at[:, lanes], og_ref, s0_ref.at[h] if has_s0 else None,
                   o_ref.at[:, lanes], sfin_ref.at[h], mask_s, layer_j=layer_j, n_valid=n_valid, first=h == 0)


def _hgrn_head(q_ref, f_ref, i_ref, g_ref, lbl_ref, og_ref, s0_ref, o_ref, sfin_ref, mask_s, *,
               layer_j, n_valid, first):
    has_s0 = s0_ref is not None
    c = GLA_CHUNK
    n_rows = q_ref.shape[0]

    n_layers = lbl_ref.shape[0]
    logits = [lbl_ref[i:i + 1, :] for i in range(n_layers)]
    mx = functools.reduce(jnp.maximum, logits)
    es = [jnp.exp(l - mx) for l in logits]
    tot = functools.reduce(lambda a, b: a + b, es)
    ws = [e / tot for e in es]
    csum = ws[0]
    for i in range(1, layer_j + 1):
        csum = csum + ws[i]
    lb = csum - ws[0]
    og = og_ref[...]

    row = lax.broadcasted_iota(jnp.int32, (c, HEAD_DIM), 0)
    tt = lax.broadcasted_iota(jnp.int32, (c, c), 0)
    ss = lax.broadcasted_iota(jnp.int32, (c, c), 1)
    n_levels = (min(n_valid, c) - 1).bit_length()
    if first:
        for lvl in range(n_levels):
            m = 1 << lvl
            pair = ((tt >> (lvl + 1)) == (ss >> (lvl + 1))) & ((tt & m) != 0) & ((ss & m) == 0)
            mask_s[lvl] = jnp.where(pair, 1.0, 0.0)
        mask_s[n_levels] = jnp.where(tt == ss, 1.0, 0.0)

    def prefix_sum(x):
        s = 1
        while s < c:
            x = x + jnp.where(row >= s, pltpu.roll(x, s, 0), 0.0)
            s *= 2
        return x

    def gates_part(fxc):
        f = lb + (1.0 - lb) * jax.nn.sigmoid(fxc)
        logf = jnp.log(jnp.maximum(f, MIN_FORGET))
        kk = 1.0 - f
        if n_valid < c:
            ok = row < n_valid
            logf = jnp.where(ok, logf, 0.0)
            kk = jnp.where(ok, kk, 0.0)
        return kk, prefix_sum(logf)

    def local_part(qc, kk, cum, vc):
        att = _nt_dot(qc, kk) * mask_s[n_levels]
        end = cum
        for lvl in range(n_levels):
            m = 1 << lvl
            if 2 * m >= SUBLANES:
                g = c // (2 * m)
                ref = jnp.broadcast_to(cum.reshape(g, 2 * m, HEAD_DIM)[:, m - 1:m, :],
                                       (g, 2 * m, HEAD_DIM)).reshape(c, HEAD_DIM)
            else:
                upper = (row & m) != 0
                ref = jnp.where(upper, pltpu.roll(end, m, 0), end)
                end = jnp.where(upper, end, pltpu.roll(end, c - m, 0))
            e = jnp.exp2(jnp.abs(cum - ref) * (-LOG2_E))
            att = att + _nt_dot(qc * e, kk * e) * mask_s[lvl]
        last = cum[c - 1:c, :]
        intra = _nn_dot(att, vc)
        q_dec = (qc * jnp.exp(cum)).astype(BF16)
        kv = _nn_dot(vc.T, kk * jnp.exp(last - cum))
        return intra, q_dec, kv, jnp.exp(last)

    def state_part(intra, q_dec, kv, dec, st, gate):
        o = _nt_dot(q_dec, st) + intra
        y = o * lax.rsqrt(jnp.mean(o * o, axis=-1, keepdims=True) + RMS_EPS) * og
        return y * _silu(gate), st * dec + kv

    st0 = s0_ref[...].T if has_s0 else jnp.zeros((HEAD_DIM, HEAD_DIM), F32)
    if n_rows < c:
        pad = jnp.zeros((c - n_rows, HEAD_DIM), F32)
        grow = lambda ref: jnp.concatenate([ref[...], pad], axis=0)
        y, st = state_part(*local_part(grow(q_ref), *gates_part(grow(f_ref)), grow(i_ref)), st0, grow(g_ref))
        o_ref[...] = y[:n_rows].astype(o_ref.dtype)
    else:
        def body(ci, st):
            rows = pl.ds(pl.multiple_of(ci * c, c), c)
            local = local_part(q_ref[rows, :], *gates_part(f_ref[rows, :]), i_ref[rows, :])
            y, st = state_part(*local, st, g_ref[rows, :])
            o_ref[rows, :] = y.astype(o_ref.dtype)
            return st

        st = lax.fori_loop(0, n_rows // c, body, st0, unroll=True)
    sfin_ref[...] = st.T


def _hgrn(proj, gate, gate_col, lb_logits, out_g, layer_j, n_seq, n_heads, d_main, s0, n_valid):
    t = proj.shape[0] // n_seq
    heads = n_heads if t < GLA_CHUNK else 1
    width = heads * HEAD_DIM
    groups = n_heads // heads
    assert gate_col % width == 0
    col = lambda first_block: (lambda b, h: (b, first_block + h))
    in_specs = [pl.BlockSpec((t, width), col(0)),
                pl.BlockSpec((t, width), col(groups)),
                pl.BlockSpec((t, width), col(2 * groups)),
                pl.BlockSpec((t, width), col(gate_col // width)),
                pl.BlockSpec((lb_logits.shape[0], width), lambda b, h: (0, h)),
                pl.BlockSpec((None, 1, HEAD_DIM), lambda b, h: (layer_j, 0, 0))]
    args = [proj, proj, proj, gate, lb_logits, out_g]
    if s0 is not None:
        in_specs.append(pl.BlockSpec((None, None, heads, HEAD_DIM, HEAD_DIM),
                                     lambda b, h: (layer_j, b, h, 0, 0)))
        args.append(s0)
    return pl.pallas_call(
        functools.partial(_hgrn_body, layer_j=layer_j, n_valid=n_valid, has_s0=s0 is not None, heads=heads),
        grid=(n_seq, groups),
        in_specs=in_specs,
        out_specs=[pl.BlockSpec((t, width), lambda b, h: (b, h)),
                   pl.BlockSpec((None, heads, HEAD_DIM, HEAD_DIM), lambda b, h: (b, h, 0, 0))],
        out_shape=[jax.ShapeDtypeStruct((n_seq * t, d_main), BF16),
                   jax.ShapeDtypeStruct((n_seq, n_heads, HEAD_DIM, HEAD_DIM), F32)],
        scratch_shapes=[pltpu.VMEM((GLA_CHUNK.bit_length(), GLA_CHUNK, GLA_CHUNK), F32)],
        compiler_params=_params("arbitrary", "arbitrary"),
        name="hgrn2",
    )(*args)


def _rope_tables(pos):
    half = HEAD_DIM // 2
    inv = ROPE_THETA ** (-jnp.arange(half, dtype=F32) / half)
    ang = pos.astype(F32)[:, None] * inv[None, :]
    cos, sin = jnp.cos(ang), jnp.sin(ang)
    return jnp.concatenate([cos, cos], axis=-1), jnp.concatenate([-sin, sin], axis=-1)


def _rope(x, cos, sin):
    return x * cos + pltpu.roll(x, HEAD_DIM // 2, 1) * sin


def _moba_body(q_ref, k_ref, v_ref, g_ref, cos_ref, sin_ref, kout_ref, vout_ref, o_ref,
               qf_s, qb_s, kb_s, vt_s, km_s, s_s, p_s):
    blk = MOBA_BLOCK
    nblk = q_ref.shape[0] // blk
    scale = HEAD_DIM ** -0.5

    t = q_ref.shape[0]
    head_slot = pl.program_id(1) % SUBLANES
    kout_rows = kout_ref.reshape(t * SUBLANES, HEAD_DIM)
    vout_rows = vout_ref.reshape(t * SUBLANES, HEAD_DIM)

    km_s[...] = jnp.zeros_like(km_s)
    for j in range(nblk):
        rows = pl.ds(j * blk, blk)
        head_rows = pl.ds(j * blk * SUBLANES + head_slot, blk, stride=SUBLANES)
        cos, sin = cos_ref[rows, :], sin_ref[rows, :]
        q = _rope(q_ref[rows, :], cos, sin)
        k = _rope(k_ref[rows, :], cos, sin)
        v = v_ref[rows, :]
        kout_rows[head_rows, :] = k
        vout_rows[head_rows, :] = v
        qf_s[j] = q
        qb_s[j] = q.astype(BF16)
        kb_s[rows, :] = k.astype(BF16)
        vt_s[:HEAD_DIM, rows] = v.T.astype(BF16)
        vt_s[HEAD_DIM:, rows] = jnp.ones((vt_s.shape[0] - HEAD_DIM, blk), BF16)
        km_s[j:j + 1, :] = jnp.sum(k, axis=0, keepdims=True) * (1.0 / blk)

    npad = km_s.shape[0]
    jidx = lax.broadcasted_iota(jnp.int32, (npad, blk), 0)
    causal = (lax.broadcasted_iota(jnp.int32, (blk, blk), 0) <=
              lax.broadcasted_iota(jnp.int32, (blk, blk), 1))

    first = lambda i: (i * (i + 1) // 2) * blk
    col_max = []
    for i in range(nblk):
        gate = lax.dot_general(km_s[...], qf_s[i], (((1,), (1,)), ((), ())),
                               precision=lax.Precision.HIGHEST, preferred_element_type=F32)
        cand = jidx < i
        gm = jnp.where(cand, gate, -jnp.inf)
        rank = jnp.zeros((npad, blk), jnp.int32)
        for jp in range(i):
            gj = gm[jp:jp + 1, :]
            rank = rank + jnp.where((gj > gm) | ((gj == gm) & (jp < jidx)), 1, 0)
        sel = jnp.where(cand & (rank < MOBA_TOPK) & (jnp.abs(gate) < jnp.inf), 1.0, 0.0)

        m = jnp.full((1, blk), -jnp.inf, F32)
        qi = qb_s[i]
        for j in range(i + 1):
            mask = causal if j == i else jnp.broadcast_to(sel[j:j + 1, :], (blk, blk)) > 0.0
            st = jnp.where(mask, _nt_dot(kb_s[pl.ds(j * blk, blk), :], qi), -jnp.inf)
            s_s[pl.ds(first(i) + j * blk, blk), :] = st
            m = jnp.maximum(m, jnp.max(st, axis=0, keepdims=True))
        col_max.append(m)
    for i in range(nblk):
        for j in range(i + 1):
            rows = pl.ds(first(i) + j * blk, blk)
            p_s[rows, :] = jnp.exp2((s_s[rows, :] - col_max[i]) * (scale * LOG2_E)).astype(BF16)
    for i in range(nblk):
        n = (i + 1) * blk
        acc = jnp.dot(vt_s[:, :n], p_s[pl.ds(first(i), n), :], preferred_element_type=F32)
        out = (acc[:HEAD_DIM] / acc[HEAD_DIM:HEAD_DIM + 1]).T
        rows = pl.ds(i * blk, blk)
        o_ref[rows, :] = (out * _silu(g_ref[rows, :])).astype(o_ref.dtype)


def _moba_prompt(proj, cos, sin, n_seq, n_heads, d_main, gate_off):
    t = proj.shape[0] // n_seq
    assert t % MOBA_BLOCK == 0
    nblk = t // MOBA_BLOCK
    hb = d_main // HEAD_DIM
    col = lambda off: (lambda b, h: (b, off + h))
    tab = pl.BlockSpec((t, HEAD_DIM), lambda b, h: (0, 0))
    assert n_heads % SUBLANES == 0
    cache_rows = pl.BlockSpec((None, t, None, SUBLANES, HEAD_DIM), lambda b, h: (b, 0, h // SUBLANES, 0, 0))
    cache_shape = jax.ShapeDtypeStruct((n_seq, t, n_heads // SUBLANES, SUBLANES, HEAD_DIM), F32)
    return pl.pallas_call(
        _moba_body,
        grid=(n_seq, n_heads),
        in_specs=[pl.BlockSpec((t, HEAD_DIM), col(0)),
                  pl.BlockSpec((t, HEAD_DIM), col(hb)),
                  pl.BlockSpec((t, HEAD_DIM), col(2 * hb)),
                  pl.BlockSpec((t, HEAD_DIM), col(gate_off // HEAD_DIM)),
                  tab, tab],
        out_specs=[cache_rows, cache_rows, pl.BlockSpec((t, HEAD_DIM), lambda b, h: (b, h))],
        out_shape=[cache_shape, cache_shape, jax.ShapeDtypeStruct((n_seq * t, d_main), BF16)],
        scratch_shapes=[pltpu.VMEM((nblk, MOBA_BLOCK, HEAD_DIM), F32),
                        pltpu.VMEM((nblk, MOBA_BLOCK, HEAD_DIM), BF16),
                        pltpu.VMEM((t, HEAD_DIM), BF16),
                        pltpu.VMEM((HEAD_DIM + 2 * SUBLANES, t), BF16),
                        pltpu.VMEM((-(-nblk // SUBLANES) * SUBLANES, HEAD_DIM), F32),
                        pltpu.VMEM((nblk * (nblk + 1) // 2 * MOBA_BLOCK, MOBA_BLOCK), F32),
                        pltpu.VMEM((nblk * (nblk + 1) // 2 * MOBA_BLOCK, MOBA_BLOCK), BF16)],
        compiler_params=_params("arbitrary", "arbitrary"),
        name="moba_prompt",
    )(proj, proj, proj, proj, cos, sin)


def _moba_parts_body(pt_ref, k0_ref, k1_ref, v0_ref, v1_ref, q_ref, kn_ref, vn_ref, cos_ref, sin_ref,
                     po_ref, pm_ref, pl_ref, pg_ref, knew_ref, s_s, qk_s, *, n_heads, n_new, n_past_blocks):
    del pt_ref
    j = pl.program_id(1)
    scale = HEAD_DIM ** -0.5
    cos, sin = cos_ref[...], sin_ref[...]
    rows = q_ref.shape[0]

    def put(ref, h, val):
        ref[:, h * HEAD_DIM:(h + 1) * HEAD_DIM] = jnp.broadcast_to(val, (rows, HEAD_DIM))

    @pl.when(j < n_past_blocks)
    def _():
        for h in range(n_heads):
            hs = slice(h * HEAD_DIM, (h + 1) * HEAD_DIM)
            hr = slice(h * rows, (h + 1) * rows)
            head_rows = pl.ds(h, PAGE_SIZE, stride=n_heads)
            kh = jnp.concatenate([k0_ref[head_rows, :], k1_ref[head_rows, :]], axis=0)
            qh = _rope(q_ref[:, hs], cos, sin)
            s_s[hr, :] = _nt_dot(qh, kh) * scale
            qk_s[hr, :] = qh * (jnp.sum(kh, axis=0, keepdims=True) * (1.0 / MOBA_BLOCK))
        s = s_s[...]
        m = jnp.max(s, axis=1, keepdims=True)
        p = jnp.exp(s - m)
        s_s[...] = p
        lsum = jnp.sum(p, axis=1, keepdims=True)
        gate = jnp.sum(qk_s[...], axis=1, keepdims=True)
        for h in range(n_heads):
            hs = slice(h * HEAD_DIM, (h + 1) * HEAD_DIM)
            hr = slice(h * rows, (h + 1) * rows)
            head_rows = pl.ds(h, PAGE_SIZE, stride=n_heads)
            vh = jnp.concatenate([v0_ref[head_rows, :], v1_ref[head_rows, :]], axis=0)
            po_ref[:, hs] = _nn_dot(s_s[hr, :], vh)
            put(pm_ref, h, m[hr])
            put(pl_ref, h, lsum[hr])
            put(pg_ref, h, gate[hr])

    @pl.when(j == n_past_blocks)
    def _():
        t_i = lax.broadcasted_iota(jnp.int32, (rows, 1), 0)
        for h in range(n_heads):
            hs = slice(h * HEAD_DIM, (h + 1) * HEAD_DIM)
            qh = _rope(q_ref[:, hs], cos, sin)
            kh = _rope(kn_ref[:, hs], cos, sin)
            vh = vn_ref[:, hs]
            knew_ref[:, hs] = kh
            logit = [jnp.where(t_i >= s, jnp.sum(qh * kh[s:s + 1, :], axis=1, keepdims=True) * scale, -jnp.inf)
                     for s in range(n_new)]
            m = functools.reduce(jnp.maximum, logit)
            ps = [jnp.exp(lg - m) for lg in logit]
            po_ref[:, hs] = functools.reduce(lambda a, b: a + b, [p * vh[s:s + 1, :] for s, p in enumerate(ps)])
            put(pm_ref, h, m)
            put(pl_ref, h, functools.reduce(lambda a, b: a + b, ps))
            put(pg_ref, h, jnp.zeros((rows, 1), F32))


def _moba_combine_body(po_ref, pm_ref, pl_ref, pg_ref, g_ref, o_ref, cur_s, sel_s, *, n_past_blocks):
    nb = n_past_blocks
    shape = cur_s.shape
    jidx = lax.broadcasted_iota(jnp.int32, shape, 0)
    cur_s[...] = pg_ref[:nb]
    sel_s[...] = jnp.zeros(shape, F32)
    for _ in range(MOBA_TOPK):
        cur = cur_s[...]
        best = jnp.max(cur, axis=0, keepdims=True)
        first = jnp.min(jnp.where(cur == best, jidx, nb), axis=0, keepdims=True)
        pick = jidx == first
        sel_s[...] = jnp.where(pick & (jnp.abs(best) < jnp.inf), 1.0, sel_s[...])
        cur_s[...] = jnp.where(pick, -jnp.inf, cur)
    sel = sel_s[...] > 0.0
    m_own = pm_ref[nb]
    m_all = jnp.maximum(m_own, jnp.max(jnp.where(sel, pm_ref[:nb], -jnp.inf), axis=0))
    w = jnp.where(sel, jnp.exp(pm_ref[:nb] - m_all[None]), 0.0)
    w_own = jnp.exp(m_own - m_all)
    den = jnp.sum(w * pl_ref[:nb], axis=0) + w_own * pl_ref[nb]
    num = jnp.sum(w * po_ref[:nb], axis=0) + w_own * po_ref[nb]
    o_ref[...] = ((num / den) * _silu(g_ref[...])).astype(o_ref.dtype)


def _moba_decode(proj, k_cache, v_cache, page_table, layer_j, cos, sin, n_seq, n_heads, d_main, gate_off, n_new):
    rows = SAMPLE_ROWS
    n_pages = page_table.shape[1]
    pages_per_block = MOBA_BLOCK // PAGE_SIZE
    assert pages_per_block == 2 and n_pages % pages_per_block == 0
    nb = n_pages // pages_per_block
    page_rows = PAGE_SIZE * n_heads

    def page(p):
        def index(b, j, pt):
            return (layer_j, pt[b, jnp.minimum(pages_per_block * j + p, n_pages - 1)], 0, 0)
        return pl.BlockSpec((None, None, page_rows, HEAD_DIM), index)

    tok = lambda off: pl.BlockSpec((rows, d_main), lambda b, j, pt: (b, off))
    tab = pl.BlockSpec((rows, HEAD_DIM), lambda b, j, pt: (0, 0))
    part = pl.BlockSpec((None, None, rows, d_main), lambda b, j, pt: (b, j, 0, 0))
    part_shape = jax.ShapeDtypeStruct((n_seq, nb + 1, rows, d_main), F32)
    po, pm, pls, pg, k_new = pl.pallas_call(
        functools.partial(_moba_parts_body, n_heads=n_heads, n_new=n_new, n_past_blocks=nb),
        grid_spec=pltpu.PrefetchScalarGridSpec(
            num_scalar_prefetch=1,
            grid=(n_seq, nb + 1),
            in_specs=[page(0), page(1), page(0), page(1), tok(0), tok(1), tok(2), tab, tab],
            out_specs=[part, part, part, part,
                       pl.BlockSpec((rows, d_main), lambda b, j, pt: (b, 0))],
            scratch_shapes=[pltpu.VMEM((n_heads * rows, MOBA_BLOCK), F32),
                            pltpu.VMEM((n_heads * rows, HEAD_DIM), F32)]),
        out_shape=[part_shape, part_shape, part_shape, part_shape,
                   jax.ShapeDtypeStruct((n_seq * rows, d_main), F32)],
        compiler_params=_params("arbitrary", "arbitrary"),
        name="moba_decode_parts",
    )(page_table, k_cache, k_cache, v_cache, v_cache, proj, proj, proj, cos, sin)

    whole = pl.BlockSpec((None, nb + 1, rows, d_main), lambda b: (b, 0, 0, 0))
    gate_blk = 1024
    assert gate_off % gate_blk == 0 and d_main % gate_blk == 0
    n_gate = d_main // gate_blk
    gate_specs = [pl.BlockSpec((rows, gate_blk), functools.partial(lambda b, i: (b, gate_off // gate_blk + i), i=i))
                  for i in range(n_gate)]

    def body(po_ref, pm_ref, pl_ref, pg_ref, *rest):
        g_refs, (o_ref, cur_s, sel_s, g_s) = rest[:n_gate], rest[n_gate:]
        for i, g_ref in enumerate(g_refs):
            g_s[:, i * gate_blk:(i + 1) * gate_blk] = g_ref[...]
        _moba_combine_body(po_ref, pm_ref, pl_ref, pg_ref, g_s, o_ref, cur_s, sel_s, n_past_blocks=nb)

    mix = pl.pallas_call(
        body,
        grid=(n_seq,),
        in_specs=[whole, whole, whole, whole] + gate_specs,
        out_specs=pl.BlockSpec((rows, d_main), lambda b: (b, 0)),
        out_shape=jax.ShapeDtypeStruct((n_seq * rows, d_main), BF16),
        scratch_shapes=[pltpu.VMEM((nb, rows, d_main), F32), pltpu.VMEM((nb, rows, d_main), F32),
                        pltpu.VMEM((rows, d_main), F32)],
        compiler_params=_params("arbitrary"),
        name="moba_decode_combine",
    )(po, pm, pls, pg, *([proj] * n_gate))
    return mix, k_new


def _xattn_body(q_ref, k_ref, v_ref, g_ref, o_ref, *, hd):
    scale = hd ** -0.5
    for h in range(q_ref.shape[-1] // hd):
        cols = slice(h * hd, (h + 1) * hd)
        s = _nt_dot(q_ref[:, cols], k_ref[:, cols]) * scale
        m = jnp.max(s, axis=-1, keepdims=True)
        p = jnp.exp(s - m)
        p = p / jnp.sum(p, axis=-1, keepdims=True)
        o_ref[:, cols] = (_nn_dot(p, v_ref[:, cols]) * _silu(g_ref[:, cols])).astype(o_ref.dtype)


def _cross_attention(proj, mem_k, mem_v, k_spec, v_spec, n_seq, tq, q_off, gate_off, d_cross, heads):
    t = proj.shape[0] // n_seq
    hd = d_cross // N_MEM_HEADS
    width = heads * hd
    nq = t // tq
    assert q_off % width == 0 and gate_off % width == 0
    row = lambda off: (lambda b, h, i: (b * nq + i, off + h))
    return pl.pallas_call(
        functools.partial(_xattn_body, hd=hd),
        grid=(n_seq, N_MEM_HEADS // heads, nq),
        in_specs=[pl.BlockSpec((tq, width), row(q_off // width)),
                  k_spec, v_spec,
                  pl.BlockSpec((tq, width), row(gate_off // width))],
        out_specs=pl.BlockSpec((tq, width), row(0)),
        out_shape=jax.ShapeDtypeStruct((n_seq * t, d_cross), BF16),
        compiler_params=_params("arbitrary", "arbitrary", "arbitrary"),
        name="mem_xattn",
    )(proj, mem_k, mem_v, proj)


def kernel(x_prompt, x_sample, cache_moba_k, cache_moba_v, state_hgrn, cache_mem_k, cache_mem_v,
           page_table, mem_prompt, norm_g, w_in, w_out, hgrn_lb_logits, hgrn_out_g, mem_norm_g,
           w_mem_kv, final_norm_g):
    bp, tp, d_model = x_prompt.shape
    bs, ts, _ = x_sample.shape
    depth = w_in.shape[0]
    d_mix = w_out.shape[1]
    d_cross = w_mem_kv.shape[2] // 2
    d_main = d_mix - d_cross
    n_heads = d_main // HEAD_DIM
    mem_len = mem_prompt.shape[1]
    mem_hd = d_cross // N_MEM_HEADS
    n_pages = page_table.shape[1]
    past_len = n_pages * PAGE_SIZE
    q_off = 3 * d_main
    gate_off = 3 * d_main + d_cross
    rows = SAMPLE_ROWS
    assert ts <= rows

    gains = norm_g.reshape(depth, 1, d_model)
    mem_gains = mem_norm_g.reshape(depth, 1, d_model)
    out_gains = hgrn_out_g.reshape(-1, 1, HEAD_DIM)

    xp = x_prompt.reshape(bp * tp, d_model)
    xs = jnp.pad(x_sample, ((0, 0), (0, rows - ts), (0, 0))).reshape(bs * rows, d_model)
    mem = mem_prompt.reshape(bp * mem_len, d_model)

    cos_p, sin_p = _rope_tables(jnp.arange(tp, dtype=jnp.int32))
    cos_s, sin_s = _rope_tables(past_len + jnp.arange(rows, dtype=jnp.int32))
    kc = cache_moba_k.reshape(cache_moba_k.shape[0], cache_moba_k.shape[1], PAGE_SIZE * n_heads, HEAD_DIM)
    vc = cache_moba_v.reshape(kc.shape)
    mem_kc = cache_mem_k.reshape(depth, bs, mem_len, d_cross)
    mem_vc = cache_mem_v.reshape(depth, bs, mem_len, d_cross)

    tm = PROJ_ROWS
    outs = {k: [] for k in ("moba_k_p", "moba_v_p", "hgrn_p", "memk_p", "memv_p", "moba_k_s", "moba_v_s", "hgrn_s")}
    for l in range(depth):
        j = l // N_MIXERS
        proj_p, proj_s = _project(_rmsnorm(xp, gains, l, BF16, NORM_ROWS), w_in, l, tm, PROJ_COLS,
                                  extra=_rmsnorm(xs, gains, l, BF16, bs * rows))
        if l % N_MIXERS == 0:
            mix_p, st_p = _hgrn(proj_p, proj_p, gate_off, hgrn_lb_logits, out_gains, j, bp, n_heads, d_main,
                                None, GLA_CHUNK)
            mix_s, st_s = _hgrn(proj_s, proj_s[:, gate_off:gate_off + d_main], 0, hgrn_lb_logits, out_gains, j, bs,
                                n_heads, d_main, state_hgrn, ts)
            outs["hgrn_p"].append(st_p)
            outs["hgrn_s"].append(st_s)
        else:
            k_p, v_p, mix_p = _moba_prompt(proj_p, cos_p, sin_p, bp, n_heads, d_main, gate_off)
            mix_s, k_s = _moba_decode(proj_s, kc, vc, page_table, j, cos_s, sin_s, bs, n_heads, d_main, gate_off, ts)
            outs["moba_k_p"].append(k_p.reshape(bp, tp, n_heads, HEAD_DIM))
            outs["moba_v_p"].append(v_p.reshape(bp, tp, n_heads, HEAD_DIM))
            outs["moba_k_s"].append(k_s.reshape(bs, rows, n_heads, HEAD_DIM)[:, :ts])
            outs["moba_v_s"].append(proj_s[:, 2 * d_main:3 * d_main].reshape(bs, rows, n_heads, HEAD_DIM)[:, :ts])
        kv = _project(_rmsnorm(mem, mem_gains, l, BF16, NORM_ROWS), w_mem_kv, l, MEM_PROJ_ROWS, PROJ_COLS)
        outs["memk_p"].append(kv[:, :d_cross].reshape(bp, mem_len, N_MEM_HEADS, mem_hd))
        outs["memv_p"].append(kv[:, d_cross:].reshape(bp, mem_len, N_MEM_HEADS, mem_hd))
        cross_p = _cross_attention(
            proj_p, kv, kv,
            pl.BlockSpec((mem_len, mem_hd), lambda b, h, i: (b, h)),
            pl.BlockSpec((mem_len, mem_hd), lambda b, h, i: (b, N_MEM_HEADS + h)),
            bp, XATTN_ROWS, q_off, gate_off + d_main, d_cross, 1)
        cached = pl.BlockSpec((None, None, mem_len, d_cross), functools.partial(lambda b, h, i, l: (l, b, 0, 0), l=l))
        cross_s = _cross_attention(proj_s, mem_kc, mem_vc, cached, cached, bs, rows, q_off, gate_off + d_main, d_cross,
                                   N_MEM_HEADS)
        xp, xs = _out_project(mix_p, cross_p, w_out, l, xp, OUT_PROJ_ROWS, OUT_PROJ_COLS, extra=(mix_s, cross_s, xs))

    final_g = final_norm_g.reshape(1, 1, d_model)
    y_prompt = _rmsnorm(xp, final_g, 0, F32, NORM_ROWS).reshape(bp, tp, d_model)
    y_sample = _rmsnorm(xs, final_g, 0, F32, bs * rows).reshape(bs, rows, d_model)[:, :ts]
    return (y_prompt, y_sample,
            jnp.stack(outs["moba_k_p"]), jnp.stack(outs["moba_v_p"]), jnp.stack(outs["hgrn_p"]),
            jnp.stack(outs["memk_p"]), jnp.stack(outs["memv_p"]),
            jnp.stack(outs["moba_k_s"]), jnp.stack(outs["moba_v_s"]), jnp.stack(outs["hgrn_s"]))
```
